```python
import numpy as np
import jax
import jax.numpy as jnp
from jax import lax

D_MODEL = 4096
BATCH = 1
SEQ = 8192
DEPTH = 1

HEAD_DIM = 128
N_HEADS = 16
N_KV_HEADS = 4
GROUP = N_HEADS // N_KV_HEADS
CMP_BLOCK = 32
CMP_STRIDE = 16
CMP_HIDDEN = 512
SEL_BLOCK = 64
N_SELECT = 16
WINDOW = 512
Q_BLOCK = 128
CONV_CH = 2048
CONV_WIDTH = 31
D_FF = 11008

NEG_INF = -1e30
FORCE = 1e9
EPS = 1e-6

Q_W = N_HEADS * HEAD_DIM
KV_W = N_KV_HEADS * HEAD_DIM
NSA_GATE_W = N_HEADS * 3
CONV_IN_W = 2 * CONV_CH
MERGE_W = 2 * D_MODEL
SPLITS = (Q_W, KV_W, KV_W, KV_W, KV_W, KV_W, KV_W, NSA_GATE_W, CONV_IN_W, MERGE_W)
N_IN = Q_W + 6 * KV_W + NSA_GATE_W + CONV_IN_W + MERGE_W

kernel_name = 'hybrid_nsa_conformer_macaron'


def rms_norm(x, g):
    xf = x.astype(jnp.float32)
    y = xf * lax.rsqrt(jnp.mean(xf * xf, axis=-1, keepdims=True) + EPS)
    return (y * g.astype(jnp.float32)).astype(x.dtype)


def layer_norm(x, g, b):
    xf = x.astype(jnp.float32)
    mu = jnp.mean(xf, axis=-1, keepdims=True)
    var = jnp.mean(jnp.square(xf - mu), axis=-1, keepdims=True)
    y = (xf - mu) * lax.rsqrt(var + EPS)
    return (y * g.astype(jnp.float32) + b.astype(jnp.float32)).astype(x.dtype)


def swiglu(h, w_gate, w_up, w_down):
    return (jax.nn.silu(h @ w_gate) * (h @ w_up)) @ w_down


def masked_softmax(s, mask):
    p = jax.nn.softmax(jnp.where(mask, s.astype(jnp.float32), NEG_INF), axis=-1)
    return jnp.where(mask, p, 0.0)


def split_heads(a):
    B, T, _ = a.shape
    return a.reshape(B, T, N_KV_HEADS, HEAD_DIM).transpose(0, 2, 1, 3)


def compress(kv, pos, w1, w2):
    B, H, T, DH = kv.shape
    n_cmp = (T - CMP_BLOCK) // CMP_STRIDE + 1
    idx = jnp.arange(n_cmp)[:, None] * CMP_STRIDE + jnp.arange(CMP_BLOCK)[None, :]
    blk = kv[:, :, idx] + pos
    flat = blk.reshape(B, H, n_cmp, CMP_BLOCK * DH)
    return jax.nn.gelu(flat @ w1) @ w2


def nsa_attention(q, k_cmp, v_cmp, k_slc, v_slc, k_win, v_win, gates):
    B, HKV, G, T, DH = q.shape
    scale = HEAD_DIM ** -0.5
    n_cmp = k_cmp.shape[2]
    n_slc = T // SEL_BLOCK
    n_top = min(N_SELECT, n_slc)
    n_blocks = T // Q_BLOCK
    cmp_end = jnp.arange(n_cmp) * CMP_STRIDE + CMP_BLOCK - 1
    cj = jnp.arange(n_cmp)[:, None] * CMP_STRIDE
    si = jnp.arange(n_slc)[None, :] * SEL_BLOCK
    overlap = jnp.clip(jnp.minimum(cj + CMP_BLOCK, si + SEL_BLOCK) - jnp.maximum(cj, si), 0, None)
    cmp_to_slc = (overlap / CMP_BLOCK).astype(jnp.float32)
    slc_ids = jnp.arange(n_slc)
    kb = k_slc.reshape(B, HKV, n_slc, SEL_BLOCK, DH)
    vb = v_slc.reshape(B, HKV, n_slc, SEL_BLOCK, DH)
    pad = ((0, 0), (0, 0), (WINDOW, 0), (0, 0))
    kw = jnp.pad(k_win, pad)
    vw = jnp.pad(v_win, pad)
    b_ix = jnp.arange(B)[:, None, None, None]
    h_ix = jnp.arange(HKV)[None, :, None, None]

    def block(i):
        start = i * Q_BLOCK
        t = start + jnp.arange(Q_BLOCK)
        qb = lax.dynamic_slice_in_dim(q, start, Q_BLOCK, axis=3)
        gb = lax.dynamic_slice_in_dim(gates, start, Q_BLOCK, axis=3)
        m_cmp = cmp_end[None, :] <= t[:, None]
        s = jnp.einsum('bhgqd,bhjd->bhgqj', qb, k_cmp) * scale
        p_cmp = masked_softmax(s, m_cmp)
        o_cmp = jnp.einsum('bhgqj,bhjd->bhgqd', p_cmp.astype(v_cmp.dtype), v_cmp)
        imp = jnp.einsum('bhgqj,js->bhqs', p_cmp, cmp_to_slc)
        cur = t // SEL_BLOCK
        forced = (slc_ids[None, :] == 0) | (slc_ids[None, :] == cur[:, None]) | (slc_ids[None, :] == cur[:, None] - 1)
        valid = slc_ids[None, :] * SEL_BLOCK <= t[:, None]
        score = jnp.where(forced, FORCE, jnp.where(valid, imp, NEG_INF))
        _, idx = lax.top_k(score, n_top)
        ks = kb[b_ix, h_ix, idx].reshape(B, HKV, Q_BLOCK, n_top * SEL_BLOCK, DH)
        vs = vb[b_ix, h_ix, idx].reshape(B, HKV, Q_BLOCK, n_top * SEL_BLOCK, DH)
        kpos = (idx[..., None] * SEL_BLOCK + jnp.arange(SEL_BLOCK)).reshape(B, HKV, Q_BLOCK, n_top * SEL_BLOCK)
        m_slc = (kpos <= t[:, None])[:, :, None]
        s = jnp.einsum('bhgqd,bhqkd->bhgqk', qb, ks) * scale
        p = masked_softmax(s, m_slc)
        o_slc = jnp.einsum('bhgqk,bhqkd->bhgqd', p.astype(vs.dtype), vs)
        kwb = lax.dynamic_slice_in_dim(kw, start, WINDOW + Q_BLOCK, axis=2)
        vwb = lax.dynamic_slice_in_dim(vw, start, WINDOW + Q_BLOCK, axis=2)
        wpos = start - WINDOW + jnp.arange(WINDOW + Q_BLOCK)
        m_win = (wpos[None, :] >= 0) & (wpos[None, :] <= t[:, None]) & (t[:, None] - wpos[None, :] < WINDOW)
        s = jnp.einsum('bhgqd,bhkd->bhgqk', qb, kwb) * scale
        p = masked_softmax(s, m_win)
        o_win = jnp.einsum('bhgqk,bhkd->bhgqd', p.astype(vwb.dtype), vwb)
        return gb[..., 0:1] * o_cmp + gb[..., 1:2] * o_slc + gb[..., 2:3] * o_win

    out = lax.map(block, jnp.arange(n_blocks))
    return out.transpose(1, 0, 4, 2, 3, 5).reshape(B, T, HKV * G * DH)


def conv_module(u_glu, conv_w, conv_b, ln_g, ln_b, w_o):
    a, g = jnp.split(u_glu, 2, axis=-1)
    u = a * jax.nn.sigmoid(g)
    u = lax.conv_general_dilated(u, conv_w[:, None, :], (1,), [(CONV_WIDTH - 1, 0)],
                                 dimension_numbers=('NWC', 'WIO', 'NWC'),
                                 feature_group_count=CONV_CH) + conv_b
    u = jax.nn.silu(layer_norm(u, ln_g, ln_b))
    return u @ w_o


def setup_inputs(seed: int = 0) -> dict:
    key = jax.random.key(seed)
    ks = iter(jax.random.split(key, 32))
    L = DEPTH

    def w(shape, fan_in):
        return jax.random.normal(next(ks), shape, jnp.float32) * fan_in ** -0.5

    def gain(shape):
        return 1.0 + 0.05 * jax.random.normal(next(ks), shape, jnp.float32)

    def small(shape, s):
        return s * jax.random.normal(next(ks), shape, jnp.float32)

    return {
        'x': jax.random.normal(next(ks), (BATCH, SEQ, D_MODEL), jnp.float32),
        'ffn1_norm': gain((L, D_MODEL)),
        'ffn1_w_gate': w((L, D_MODEL, D_FF), D_MODEL),
        'ffn1_w_up': w((L, D_MODEL, D_FF), D_MODEL),
        'ffn1_w_down': w((L, D_FF, D_MODEL), D_FF),
        'mix_norm': gain((L, D_MODEL)),
        'w_in': w((L, D_MODEL, N_IN), D_MODEL),
        'q_norm': gain((L, HEAD_DIM)),
        'k_norm': gain((L, 3, HEAD_DIM)),
        'cmp_pos_k': small((L, CMP_BLOCK, HEAD_DIM), 0.1),
        'cmp_k_w1': w((L, CMP_BLOCK * HEAD_DIM, CMP_HIDDEN), CMP_BLOCK * HEAD_DIM),
        'cmp_k_w2': w((L, CMP_HIDDEN, HEAD_DIM), CMP_HIDDEN),
        'cmp_pos_v': small((L, CMP_BLOCK, HEAD_DIM), 0.1),
        'cmp_v_w1': w((L, CMP_BLOCK * HEAD_DIM, CMP_HIDDEN), CMP_BLOCK * HEAD_DIM),
        'cmp_v_w2': w((L, CMP_HIDDEN, HEAD_DIM), CMP_HIDDEN),
        'nsa_w_o': w((L, Q_W, D_MODEL), Q_W),
        'conv_w': w((L, CONV_WIDTH, CONV_CH), CONV_WIDTH),
        'conv_b': small((L, CONV_CH), 0.02),
        'conv_ln_g': gain((L, CONV_CH)),
        'conv_ln_b': small((L, CONV_CH), 0.02),
        'conv_w_o': w((L, CONV_CH, D_MODEL), CONV_CH),
        'w_out': w((L, D_MODEL, D_MODEL), D_MODEL),
        'ffn2_norm': gain((L, D_MODEL)),
        'ffn2_w_gate': w((L, D_MODEL, D_FF), D_MODEL),
        'ffn2_w_up': w((L, D_MODEL, D_FF), D_MODEL),
        'ffn2_w_down': w((L, D_FF, D_MODEL), D_FF),
    }


def reference(x, ffn1_norm, ffn1_w_gate, ffn1_w_up, ffn1_w_down, mix_norm, w_in, q_norm, k_norm,
              cmp_pos_k, cmp_k_w1, cmp_k_w2, cmp_pos_v, cmp_v_w1, cmp_v_w2, nsa_w_o,
              conv_w, conv_b, conv_ln_g, conv_ln_b, conv_w_o, w_out,
              ffn2_norm, ffn2_w_gate, ffn2_w_up, ffn2_w_down):
    B, T, _ = x.shape
    offs = [int(o) for o in np.cumsum(SPLITS)[:-1]]
    for l in range(DEPTH):
        x = x + 0.5 * swiglu(rms_norm(x, ffn1_norm[l]), ffn1_w_gate[l], ffn1_w_up[l], ffn1_w_down[l])
        h = rms_norm(x, mix_norm[l])
        proj = h @ w_in[l]
        q, k_c, v_c, k_s, v_s, k_w, v_w, g_nsa, u_glu, g_mrg = jnp.split(proj, offs, axis=-1)
        q = rms_norm(q.reshape(B, T, N_HEADS, HEAD_DIM), q_norm[l])
        q = q.reshape(B, T, N_KV_HEADS, GROUP, HEAD_DIM).transpose(0, 2, 3, 1, 4)
        k_cmp = rms_norm(compress(split_heads(k_c), cmp_pos_k[l], cmp_k_w1[l], cmp_k_w2[l]), k_norm[l, 0])
        v_cmp = compress(split_heads(v_c), cmp_pos_v[l], cmp_v_w1[l], cmp_v_w2[l])
        k_slc = rms_norm(split_heads(k_s), k_norm[l, 1])
        k_win = rms_norm(split_heads(k_w), k_norm[l, 2])
        gates = jax.nn.sigmoid(g_nsa).reshape(B, T, N_KV_HEADS, GROUP, 3).transpose(0, 2, 3, 1, 4)
        y_nsa = nsa_attention(q, k_cmp, v_cmp, k_slc, split_heads(v_s), k_win, split_heads(v_w), gates) @ nsa_w_o[l]
        y_conv = conv_module(u_glu, conv_w[l], conv_b[l], conv_ln_g[l], conv_ln_b[l], conv_w_o[l])
        g = jax.nn.sigmoid(g_mrg).reshape(B, T, 2, D_MODEL)
        x = x + (g[:, :, 0] * y_nsa + g[:, :, 1] * y_conv) @ w_out[l]
        x = x + 0.5 * swiglu(rms_norm(x, ffn2_norm[l]), ffn2_w_gate[l], ffn2_w_up[l], ffn2_w_down[l])
    return x
```

```python
import functools

import numpy as np
import jax
import jax.numpy as jnp
from jax import lax
from jax.experimental import pallas as pl
from jax.experimental.pallas import tpu as pltpu

HEAD_DIM = 128
N_HEADS = 16
N_KV_HEADS = 4
GROUP = N_HEADS // N_KV_HEADS
CMP_BLOCK = 32
CMP_STRIDE = 16
SEL_BLOCK = 64
N_SELECT = 16
WINDOW = 512
CONV_WIDTH = 31
NEG_INF = -1e30
FORCE = 1e9
EPS = 1e-6

Q_W = N_HEADS * HEAD_DIM
KV_W = N_KV_HEADS * HEAD_DIM
NSA_GATE_W = N_HEADS * 3

MXU_DTYPE = jnp.bfloat16
LANES = 128
V7X_VMEM_BYTES = 64 * 1024 * 1024
VMEM_CAP = V7X_VMEM_BYTES - 6 * 1024 * 1024

F32 = jnp.float32


def _nbytes(shape, dtype):
    return int(np.prod(shape)) * jnp.dtype(dtype).itemsize


def _params(semantics, blocks, extra=0):
    need = 2 * sum(_nbytes(s, d) for s, d in blocks) + extra + (2 << 20)
    return pltpu.CompilerParams(dimension_semantics=semantics,
                                vmem_limit_bytes=int(min(max(need, 16 << 20), VMEM_CAP)))


def _tile(n, pref):
    return pref if n % pref == 0 else n


def _rmsnorm_kernel(x_ref, g_ref, o_ref):
    x = x_ref[...]
    ms = jnp.mean(x * x, axis=-1, keepdims=True)
    o_ref[...] = (x * lax.rsqrt(ms + EPS) * g_ref[...]).astype(o_ref.dtype)


def rmsnorm_cast(x, gain):
    m, d = x.shape
    tm = _tile(m, 256)
    return pl.pallas_call(
        _rmsnorm_kernel,
        grid=(m // tm,),
        in_specs=[pl.BlockSpec((tm, d), lambda i: (i, 0)),
                  pl.BlockSpec((1, d), lambda i: (0, 0))],
        out_specs=pl.BlockSpec((tm, d), lambda i: (i, 0)),
        out_shape=jax.ShapeDtypeStruct((m, d), MXU_DTYPE),
        compiler_params=_params(("parallel",), [((tm, d), F32), ((tm, d), MXU_DTYPE)],
                                extra=2 * _nbytes((tm, d), F32)),
        name="rmsnorm_cast",
    )(x, gain.reshape(1, d).astype(F32))


def _dual_kernel(x_ref, w1_ref, w2_ref, o_ref, *, act):
    x = x_ref[...]
    a = jnp.dot(x, w1_ref[...], preferred_element_type=F32)
    b = jnp.dot(x, w2_ref[...], preferred_element_type=F32)
    o_ref[...] = (act(a) * b).astype(o_ref.dtype)


def dual_mm(x, w1, w2, act, out_dtype, name):
    m, k = x.shape
    n = w1.shape[1]
    tm, tn = _tile(m, 1024), _tile(n, 512)
    return pl.pallas_call(
        functools.partial(_dual_kernel, act=act),
        grid=(m // tm, n // tn),
        in_specs=[pl.BlockSpec((tm, k), lambda i, j: (i, 0)),
                  pl.BlockSpec((k, tn), lambda i, j: (0, j)),
                  pl.BlockSpec((k, tn), lambda i, j: (0, j))],
        out_specs=pl.BlockSpec((tm, tn), lambda i, j: (i, j)),
        out_shape=jax.ShapeDtypeStruct((m, n), out_dtype),
        compiler_params=_params(("parallel", "arbitrary"),
                                [((tm, k), x.dtype), ((k, tn), w1.dtype), ((k, tn), w2.dtype),
                                 ((tm, tn), out_dtype)],
                                extra=4 * _nbytes((tm, tn), F32)),
        name=name,
    )(x, w1, w2)


def _mm_kernel(x_ref, w_ref, g_ref, o_ref, *, mode):
    acc = jnp.dot(x_ref[...], w_ref[...], preferred_element_type=F32)
    if mode == "headnorm":
        for c in range(acc.shape[1] // HEAD_DIM):
            sl = slice(c * HEAD_DIM, (c + 1) * HEAD_DIM)
            v = acc[:, sl]
            ms = jnp.mean(v * v, axis=-1, keepdims=True)
            o_ref[:, sl] = (v * lax.rsqrt(ms + EPS) * g_ref[:, sl]).astype(o_ref.dtype)
    elif mode == "sigmoid":
        o_ref[...] = jax.nn.sigmoid(acc).astype(o_ref.dtype)
    else:
        o_ref[...] = acc.astype(o_ref.dtype)


def mm(x, w, mode, out_dtype, name, gain=None):
    m, k = x.shape
    n = w.shape[1]
    tm, tn = _tile(m, 1024), _tile(n, 512)
    if gain is None:
        gain = jnp.ones((1, n), F32)
    return pl.pallas_call(
        functools.partial(_mm_kernel, mode=mode),
        grid=(m // tm, n // tn),
        in_specs=[pl.BlockSpec((tm, k), lambda i, j: (i, 0)),
                  pl.BlockSpec((k, tn), lambda i, j: (0, j)),
                  pl.BlockSpec((1, tn), lambda i, j: (0, j))],
        out_specs=pl.BlockSpec((tm, tn), lambda i, j: (i, j)),
        out_shape=jax.ShapeDtypeStruct((m, n), out_dtype),
        compiler_params=_params(("parallel", "arbitrary"),
                                [((tm, k), x.dtype), ((k, tn), w.dtype), ((tm, tn), out_dtype)],
                                extra=3 * _nbytes((tm, tn), F32)),
        name=name,
    )(x, w, gain.reshape(1, n).astype(F32))


def _mm_acc_kernel(h_ref, w_ref, r_ref, o_ref, acc_ref, *, scale, nk):
    kk = pl.program_id(2)

    @pl.when(kk == 0)
    def _():
        acc_ref[...] = jnp.zeros_like(acc_ref)

    acc_ref[...] += jnp.dot(h_ref[...], w_ref[...], preferred_element_type=F32)

    @pl.when(kk == nk - 1)
    def _():
        o_ref[...] = (r_ref[...] + scale * acc_ref[...]).astype(o_ref.dtype)


def mm_resid(h, w, resid, scale, tk, name):
    m, k = h.shape
    n = w.shape[1]
    tm, tn = _tile(m, 1024), _tile(n, 1024)
    tk = _tile(k, tk)
    nk = k // tk
    return pl.pallas_call(
        functools.partial(_mm_acc_kernel, scale=scale, nk=nk),
        grid=(m // tm, n // tn, nk),
        in_specs=[pl.BlockSpec((tm, tk), lambda i, j, kk: (i, kk)),
                  pl.BlockSpec((tk, tn), lambda i, j, kk: (kk, j)),
                  pl.BlockSpec((tm, tn), lambda i, j, kk: (i, j))],
        out_specs=pl.BlockSpec((tm, tn), lambda i, j, kk: (i, j)),
        out_shape=jax.ShapeDtypeStruct((m, n), F32),
        scratch_shapes=[pltpu.VMEM((tm, tn), F32)],
        compiler_params=_params(("parallel", "arbitrary", "arbitrary"),
                                [((tm, tk), h.dtype), ((tk, tn), w.dtype), ((tm, tn), F32),
                                 ((tm, tn), F32)],
                                extra=3 * _nbytes((tm, tn), F32)),
        name=name,
    )(h, w, resid)


def _gated_kernel(*refs, has_prev):
    if has_prev:
        a_ref, wo_ref, h_ref, wm_ref, p_ref, o_ref = refs
    else:
        a_ref, wo_ref, h_ref, wm_ref, o_ref = refs
    y = jnp.dot(a_ref[...], wo_ref[...], preferred_element_type=F32)
    g = jax.nn.sigmoid(jnp.dot(h_ref[...], wm_ref[...], preferred_element_type=F32))
    r = g * y
    if has_prev:
        r = p_ref[...] + r
    o_ref[...] = r.astype(o_ref.dtype)


def gated_mm(a, wo, h, wm, prev, out_dtype, name):
    m, ka = a.shape
    kh = h.shape[1]
    n = wo.shape[1]
    tm, tn = _tile(m, 512), _tile(n, 512)
    has_prev = prev is not None
    in_specs = [pl.BlockSpec((tm, ka), lambda i, j: (i, 0)),
                pl.BlockSpec((ka, tn), lambda i, j: (0, j)),
                pl.BlockSpec((tm, kh), lambda i, j: (i, 0)),
                pl.BlockSpec((kh, tn), lambda i, j: (0, j))]
    blocks = [((tm, ka), a.dtype), ((ka, tn), wo.dtype), ((tm, kh), h.dtype), ((kh, tn), wm.dtype),
              ((tm, tn), out_dtype)]
    args = [a, wo, h, wm]
    if has_prev:
        in_specs.append(pl.BlockSpec((tm, tn), lambda i, j: (i, j)))
        blocks.append(((tm, tn), prev.dtype))
        args.append(prev)
    return pl.pallas_call(
        functools.partial(_gated_kernel, has_prev=has_prev),
        grid=(m // tm, n // tn),
        in_specs=in_specs,
        out_specs=pl.BlockSpec((tm, tn), lambda i, j: (i, j)),
        out_shape=jax.ShapeDtypeStruct((m, n), out_dtype),
        compiler_params=_params(("parallel", "arbitrary"), blocks, extra=4 * _nbytes((tm, tn), F32)),
        name=name,
    )(*args)


def _gelu_tanh(x):
    return 0.5 * x * (1.0 + jnp.tanh(np.sqrt(2.0 / np.pi).astype(np.float32) * (x + 0.044715 * (x * x * x))))


def _compress_kernel(a_ref, pa_ref, pb_ref, w1a_ref, w1b_ref, w2_ref, g_ref, o_ref, *, norm):
    a = a_ref[0]
    nch = a.shape[0]
    ua = jnp.dot((a + pa_ref[...]).astype(MXU_DTYPE), w1a_ref[...], preferred_element_type=F32)
    ub = jnp.dot((a + pb_ref[...]).astype(MXU_DTYPE), w1b_ref[...], preferred_element_type=F32)
    pre = ua + pltpu.roll(ub, nch - 1, 0)
    hid = _gelu_tanh(pre).astype(MXU_DTYPE)
    out = jnp.dot(hid, w2_ref[...], preferred_element_type=F32)
    if norm:
        ms = jnp.mean(out * out, axis=-1, keepdims=True)
        out = out * lax.rsqrt(ms + EPS) * g_ref[...]
    o_ref[0] = out.astype(o_ref.dtype)


def compress(chunks, pos, w1, w2, gain, norm, name):
    hkv, nch, cw = chunks.shape
    hid = w1.shape[1]
    posf = pos.reshape(1, CMP_BLOCK * HEAD_DIM).astype(F32)
    pa, pb = posf[:, :cw], posf[:, cw:]
    w1c = w1.astype(MXU_DTYPE)
    w1a, w1b = w1c[:cw], w1c[cw:]
    full = lambda shape: pl.BlockSpec(shape, lambda h: (0,) * len(shape))
    return pl.pallas_call(
        functools.partial(_compress_kernel, norm=norm),
        grid=(hkv,),
        in_specs=[pl.BlockSpec((1, nch, cw), lambda h: (h, 0, 0)),
                  full((1, cw)), full((1, cw)), full((cw, hid)), full((cw, hid)),
                  full((hid, HEAD_DIM)), full((1, HEAD_DIM))],
        out_specs=pl.BlockSpec((1, nch, HEAD_DIM), lambda h: (h, 0, 0)),
        out_shape=jax.ShapeDtypeStruct((hkv, nch, HEAD_DIM), MXU_DTYPE),
        compiler_params=_params(("parallel",),
                                [((nch, cw), F32), ((cw, hid), MXU_DTYPE), ((cw, hid), MXU_DTYPE)],
                                extra=8 * _nbytes((nch, cw), F32)),
        name=name,
    )(chunks, pa, pb, w1a, w1b, w2.astype(MXU_DTYPE), gain.reshape(1, HEAD_DIM).astype(F32))


def _conv_kernel(cur_ref, prev_ref, cw_ref, cb_ref, lg_ref, lb_ref, o_ref, ext_ref, y_ref, *, halo, rc, cc):
    i = pl.program_id(0)
    tm, ch = cur_ref.shape
    ext_ref[0:halo, :] = jnp.where(i > 0, prev_ref[tm - halo:tm, :], 0.0)
    ext_ref[halo:halo + tm, :] = cur_ref[...]
    base = halo - (CONV_WIDTH - 1)
    for c0 in range(0, ch, cc):
        for r0 in range(0, tm, rc):
            acc = jnp.zeros((rc, cc), F32) + cb_ref[:, c0:c0 + cc]
            for w in range(CONV_WIDTH):
                acc = acc + ext_ref[base + r0 + w:base + r0 + w + rc, c0:c0 + cc] * cw_ref[w:w + 1, c0:c0 + cc]
            y_ref[r0:r0 + rc, c0:c0 + cc] = acc
    y = y_ref[...]
    mu = jnp.mean(y, axis=-1, keepdims=True)
    d = y - mu
    var = jnp.mean(d * d, axis=-1, keepdims=True)
    z = d * lax.rsqrt(var + EPS) * lg_ref[...] + lb_ref[...]
    o_ref[...] = (z * jax.nn.sigmoid(z)).astype(o_ref.dtype)


def conv_module(u, conv_w, conv_b, ln_g, ln_b):
    t, ch = u.shape
    tm = _tile(t, 256)
    halo = 32
    row = lambda v: v.reshape(1, ch).astype(F32)
    full = lambda shape: pl.BlockSpec(shape, lambda i: (0, 0))
    return pl.pallas_call(
        functools.partial(_conv_kernel, halo=halo, rc=64, cc=min(512, ch)),
        grid=(t // tm,),
        in_specs=[pl.BlockSpec((tm, ch), lambda i: (i, 0)),
                  pl.BlockSpec((tm, ch), lambda i: (jnp.maximum(i - 1, 0), 0)),
                  full((CONV_WIDTH, ch)), full((1, ch)), full((1, ch)), full((1, ch))],
        out_specs=pl.BlockSpec((tm, ch), lambda i: (i, 0)),
        out_shape=jax.ShapeDtypeStruct((t, ch), MXU_DTYPE),
        scratch_shapes=[pltpu.VMEM((tm + halo, ch), F32), pltpu.VMEM((tm, ch), F32)],
        compiler_params=_params(("parallel",),
                                [((tm, ch), F32), ((tm, ch), F32), ((tm, ch), MXU_DTYPE)],
                                extra=6 * _nbytes((tm, ch), F32)),
        name="conv_module",
    )(u, u, conv_w.astype(F32), row(conv_b), row(ln_g), row(ln_b))


def _dot_t(a, b):
    return lax.dot_general(a, b, (((0,), (0,)), ((), ())), preferred_element_type=F32)


def _nsa_kernel(q_ref, kc_ref, vc_ref, ks_ref, kw_ref, vs_ref, vw_ref, gt_ref, mt_ref, o_ref,
                qt_ref, score_ref, sel_ref, m_ref, l_ref, acc_ref, *, tq, tk, n_top):
    i = pl.program_id(1)
    t0 = i * tq
    r = GROUP * tq
    nc = kc_ref.shape[1]
    ns = mt_ref.shape[0]
    nb = tk // SEL_BLOCK

    def lanes4(v):
        return jnp.concatenate([v] * GROUP, axis=1)

    for g in range(GROUP):
        qg = q_ref[:, g * HEAD_DIM:(g + 1) * HEAD_DIM].astype(F32)
        qt_ref[:, g * tq:(g + 1) * tq] = qg.T.astype(MXU_DTYPE)
    qt = qt_ref[...]
    t_q = t0 + lax.broadcasted_iota(jnp.int32, (1, tq), 1)
    t_r = lanes4(t_q)

    s = jnp.dot(kc_ref[0], qt, preferred_element_type=F32)
    cmp_end = lax.broadcasted_iota(jnp.int32, (nc, 1), 0) * CMP_STRIDE + (CMP_BLOCK - 1)
    msk = cmp_end <= t_r
    sm = jnp.where(msk, s, NEG_INF)
    mx = jnp.max(sm, axis=0, keepdims=True)
    e = jnp.where(msk, jnp.exp(sm - mx), 0.0)
    den = jnp.sum(e, axis=0, keepdims=True)
    p = e / jnp.where(den > 0.0, den, 1.0)
    o_cmp = _dot_t(vc_ref[0], p.astype(MXU_DTYPE))

    psum = p[:, 0:tq]
    for g in range(1, GROUP):
        psum = psum + p[:, g * tq:(g + 1) * tq]
    p_hi = psum.astype(MXU_DTYPE)
    p_lo = (psum - p_hi.astype(F32)).astype(MXU_DTYPE)
    imp = (jnp.dot(mt_ref[...], p_hi, preferred_element_type=F32)
           + jnp.dot(mt_ref[...], p_lo, preferred_element_type=F32))
    s_id = lax.broadcasted_iota(jnp.int32, (ns, 1), 0)
    cur = t_q // SEL_BLOCK
    forced = (s_id == 0) | (s_id == cur) | (s_id == cur - 1)
    valid = s_id * SEL_BLOCK <= t_q
    score = jnp.where(forced, FORCE, jnp.where(valid, imp, NEG_INF))
    score_ref[...] = score

    def rank_body(sp, cnt):
        row = score_ref[pl.ds(sp, 1), :]
        ahead = (row > score) | ((row == score) & (sp < s_id))
        return cnt + ahead.astype(F32)

    cnt = lax.fori_loop(0, ns, rank_body, jnp.zeros((ns, tq), F32))
    sel_ref[...] = (cnt < n_top).astype(F32)

    def attend(k_ref, v_ref, lo, hi, mask_fn):
        m_ref[...] = jnp.full(m_ref.shape, NEG_INF, F32)
        l_ref[...] = jnp.zeros(l_ref.shape, F32)
        acc_ref[...] = jnp.zeros(acc_ref.shape, F32)

        def body(kt, carry):
            k0 = pl.multiple_of(kt * tk, tk)
            st = jnp.dot(k_ref[pl.ds(k0, tk), :], qt, preferred_element_type=F32)
            kpos = k0 + lax.broadcasted_iota(jnp.int32, (tk, 1), 0)
            mk = mask_fn(kt, kpos)
            sm_ = jnp.where(mk, st, NEG_INF)
            m_old = m_ref[...]
            m_new = jnp.maximum(m_old, jnp.max(sm_, axis=0, keepdims=True))
            alpha = jnp.exp(m_old - m_new)
            pt = jnp.where(mk, jnp.exp(sm_ - m_new), 0.0)
            l_ref[...] = alpha * l_ref[...] + jnp.sum(pt, axis=0, keepdims=True)
            acc_ref[...] = alpha * acc_ref[...] + _dot_t(v_ref[pl.ds(k0, tk), :], pt.astype(MXU_DTYPE))
            m_ref[...] = m_new
            return carry

        lax.fori_loop(lo, hi, body, 0)
        l = l_ref[...]
        return jnp.where(l > 0.0, acc_ref[...] / jnp.where(l > 0.0, l, 1.0), 0.0)

    def slc_mask(kt, kpos):
        rows = [jnp.broadcast_to(sel_ref[pl.ds(kt * nb + b, 1), :], (SEL_BLOCK, tq)) for b in range(nb)]
        picked = lanes4(jnp.concatenate(rows, axis=0) if nb > 1 else rows[0])
        return (picked > 0.0) & (kpos <= t_r)

    def win_mask(kt, kpos):
        return (kpos <= t_r) & (t_r - kpos < WINDOW)

    hi = (t0 + tq + tk - 1) // tk
    o_slc = attend(ks_ref, vs_ref, 0, hi, slc_mask)
    o_win = attend(kw_ref, vw_ref, jnp.maximum(t0 - (WINDOW - 1), 0) // tk, hi, win_mask)

    def gate(b):
        return jnp.concatenate([gt_ref[0, g * 3 + b:g * 3 + b + 1, :] for g in range(GROUP)], axis=1)

    out_t = gate(0) * o_cmp + gate(1) * o_slc + gate(2) * o_win
    for g in range(GROUP):
        o_ref[:, g * HEAD_DIM:(g + 1) * HEAD_DIM] = out_t[:, g * tq:(g + 1) * tq].T.astype(o_ref.dtype)


def nsa_attention(q, k_cmp, v_cmp, k_sw, v_sw, gates_t, tq=128, tk=128):
    t = q.shape[0]
    tq, tk = min(tq, t), min(tk, t)
    nc = k_cmp.shape[1]
    ns = t // SEL_BLOCK
    n_top = min(N_SELECT, ns)
    r = GROUP * tq
    cj = np.arange(nc)[None, :] * CMP_STRIDE
    si = np.arange(ns)[:, None] * SEL_BLOCK
    ov = np.clip(np.minimum(cj + CMP_BLOCK, si + SEL_BLOCK) - np.maximum(cj, si), 0, None) / CMP_BLOCK
    ov[:, (t - CMP_BLOCK) // CMP_STRIDE + 1:] = 0.0
    mt = jnp.asarray(ov, MXU_DTYPE)

    head = lambda off: pl.BlockSpec((t, HEAD_DIM), lambda h, i: (0, off + h))
    return pl.pallas_call(
        functools.partial(_nsa_kernel, tq=tq, tk=tk, n_top=n_top),
        grid=(N_KV_HEADS, t // tq),
        in_specs=[pl.BlockSpec((tq, GROUP * HEAD_DIM), lambda h, i: (i, h)),
                  pl.BlockSpec((1, nc, HEAD_DIM), lambda h, i: (h, 0, 0)),
                  pl.BlockSpec((1, nc, HEAD_DIM), lambda h, i: (h, 0, 0)),
                  head(0), head(N_KV_HEADS), head(0), head(N_KV_HEADS),
                  pl.BlockSpec((1, 16, tq), lambda h, i: (h, 0, i)),
                  pl.BlockSpec((ns, nc), lambda h, i: (0, 0))],
        out_specs=pl.BlockSpec((tq, GROUP * HEAD_DIM), lambda h, i: (i, h)),
        out_shape=jax.ShapeDtypeStruct((t, Q_W), MXU_DTYPE),
        scratch_shapes=[pltpu.VMEM((HEAD_DIM, r), MXU_DTYPE),
                        pltpu.VMEM((ns, tq), F32), pltpu.VMEM((ns, tq), F32),
                        pltpu.VMEM((1, r), F32), pltpu.VMEM((1, r), F32),
                        pltpu.VMEM((HEAD_DIM, r), F32)],
        compiler_params=_params(("parallel", "arbitrary"),
                                [((t, HEAD_DIM), MXU_DTYPE)] * 4 + [((nc, HEAD_DIM), MXU_DTYPE)] * 2
                                + [((tq, GROUP * HEAD_DIM), MXU_DTYPE)] * 2 + [((ns, nc), MXU_DTYPE)],
                                extra=12 * _nbytes((max(nc, tk), r), F32)),
        name="nsa_attention",
    )(q, k_cmp, v_cmp, k_sw, k_sw, v_sw, v_sw, gates_t, mt)


def _pad_cols(w, n):
    return jnp.pad(w, ((0, 0), (0, n - w.shape[1])))


def _ffn(x, norm_g, w_gate, w_up, w_down):
    f = w_gate.shape[1]
    fp = -(-f // 1024) * 1024 if f > 1024 else f
    wg = _pad_cols(w_gate.astype(MXU_DTYPE), fp)
    wu = _pad_cols(w_up.astype(MXU_DTYPE), fp)
    wd = jnp.pad(w_down.astype(MXU_DTYPE), ((0, fp - f), (0, 0)))
    xn = rmsnorm_cast(x, norm_g)
    h = dual_mm(xn, wg, wu, jax.nn.silu, MXU_DTYPE, "ffn_gate_up")
    tk = fp // 4 if (fp // 4) % LANES == 0 else fp
    return mm_resid(h, wd, x, 0.5, tk, "ffn_down")


def _layer(x, p):
    t, d = x.shape
    x = _ffn(x, p["ffn1_norm"], p["ffn1_w_gate"], p["ffn1_w_up"], p["ffn1_w_down"])

    hn = rmsnorm_cast(x, p["mix_norm"])
    w_in = p["w_in"].astype(MXU_DTYPE)
    conv_ch = p["conv_w"].shape[1]
    o = 0
    wq = w_in[:, o:o + Q_W]; o += Q_W
    wkc = w_in[:, o:o + KV_W]; o += KV_W
    wvc = w_in[:, o:o + KV_W]; o += KV_W
    wks = w_in[:, o:o + KV_W]; o += KV_W
    wvs = w_in[:, o:o + KV_W]; o += KV_W
    wkw = w_in[:, o:o + KV_W]; o += KV_W
    wvw = w_in[:, o:o + KV_W]; o += KV_W
    wgn = w_in[:, o:o + NSA_GATE_W]; o += NSA_GATE_W
    wga = w_in[:, o:o + conv_ch]; o += conv_ch
    wgg = w_in[:, o:o + conv_ch]; o += conv_ch
    wm0 = w_in[:, o:o + d]; o += d
    wm1 = w_in[:, o:o + d]; o += d

    scale = HEAD_DIM ** -0.5
    q = mm(hn, wq, "headnorm", MXU_DTYPE, "proj_q", gain=jnp.tile(p["q_norm"] * scale, N_HEADS))
    k_sw = mm(hn, jnp.concatenate([wks, wkw], axis=1), "headnorm", MXU_DTYPE, "proj_k",
              gain=jnp.concatenate([jnp.tile(p["k_norm"][1], N_KV_HEADS), jnp.tile(p["k_norm"][2], N_KV_HEADS)]))
    v_sw = mm(hn, jnp.concatenate([wvs, wvw], axis=1), "plain", MXU_DTYPE, "proj_v")
    kv_c = mm(hn, jnp.concatenate([wkc, wvc], axis=1), "plain", F32, "proj_kv_cmp")
    g_nsa = mm(hn, _pad_cols(wgn, LANES), "sigmoid", F32, "proj_gates")
    u = dual_mm(hn, wgg, wga, jax.nn.sigmoid, F32, "proj_glu")

    chunks = kv_c.reshape(t // CMP_STRIDE, CMP_STRIDE, 2, N_KV_HEADS, HEAD_DIM)
    chunks = chunks.transpose(2, 3, 0, 1, 4).reshape(2, N_KV_HEADS, t // CMP_STRIDE, CMP_STRIDE * HEAD_DIM)
    k_cmp = compress(chunks[0], p["cmp_pos_k"], p["cmp_k_w1"], p["cmp_k_w2"], p["k_norm"][0], True, "compress_k")
    v_cmp = compress(chunks[1], p["cmp_pos_v"], p["cmp_v_w1"], p["cmp_v_w2"], p["k_norm"][0], False, "compress_v")

    gates_t = g_nsa[:, :NSA_GATE_W].reshape(t, N_KV_HEADS, GROUP * 3).transpose(1, 2, 0)
    gates_t = jnp.pad(gates_t, ((0, 0), (0, 16 - GROUP * 3), (0, 0)))
    attn = nsa_attention(q, k_cmp, v_cmp, k_sw, v_sw, gates_t)

    uc = conv_module(u, p["conv_w"], p["conv_b"], p["conv_ln_g"], p["conv_ln_b"])

    mix = gated_mm(attn, p["nsa_w_o"].astype(MXU_DTYPE), hn, wm0, None, F32, "merge_nsa")
    mix = gated_mm(uc, p["conv_w_o"].astype(MXU_DTYPE), hn, wm1, mix, MXU_DTYPE, "merge_conv")
    x = mm_resid(mix, p["w_out"].astype(MXU_DTYPE), x, 1.0, d, "mix_out")

    return _ffn(x, p["ffn2_norm"], p["ffn2_w_gate"], p["ffn2_w_up"], p["ffn2_w_down"])


_NAMES = ("ffn1_norm", "ffn1_w_gate", "ffn1_w_up", "ffn1_w_down", "mix_norm", "w_in", "q_norm", "k_norm",
          "cmp_pos_k", "cmp_k_w1", "cmp_k_w2", "cmp_pos_v", "cmp_v_w1", "cmp_v_w2", "nsa_w_o",
          "conv_w", "conv_b", "conv_ln_g", "conv_ln_b", "conv_w_o", "w_out",
          "ffn2_norm", "ffn2_w_gate", "ffn2_w_up", "ffn2_w_down")


def kernel(x, ffn1_norm, ffn1_w_gate, ffn1_w_up, ffn1_w_down, mix_norm, w_in, q_norm, k_norm, cmp_pos_k, cmp_k_w1, cmp_k_w2, cmp_pos_v, cmp_v_w1, cmp_v_w2, nsa_w_o, conv_w, conv_b, conv_ln_g, conv_ln_b, conv_w_o, w_out, ffn2_norm, ffn2_w_gate, ffn2_w_up, ffn2_w_down):
    params = (ffn1_norm, ffn1_w_gate, ffn1_w_up, ffn1_w_down, mix_norm, w_in, q_norm, k_norm, cmp_pos_k,
              cmp_k_w1, cmp_k_w2, cmp_pos_v, cmp_v_w1, cmp_v_w2, nsa_w_o, conv_w, conv_b, conv_ln_g,
              conv_ln_b, conv_w_o, w_out, ffn2_norm, ffn2_w_gate, ffn2_w_up, ffn2_w_down)
    b, t, d = x.shape
    depth = ffn1_norm.shape[0]
    xs = x.reshape(b * t, d)
    outs = []
    for bi in range(b):
        xb = xs[bi * t:(bi + 1) * t]
        for l in range(depth):
            xb = _layer(xb, {n: a[l] for n, a in zip(_NAMES, params)})
        outs.append(xb)
    out = outs[0] if b == 1 else jnp.concatenate(outs, axis=0)
    return out.reshape(b, t, d)
```

```python
import functools

import numpy as np
import jax
import jax.numpy as jnp
from jax import lax
from jax.experimental import pallas as pl
from jax.experimental.pallas import tpu as pltpu

HEAD_DIM = 128
N_HEADS = 16
N_KV_HEADS = 4
GROUP = N_HEADS // N_KV_HEADS
CMP_BLOCK = 32
CMP_STRIDE = 16
SEL_BLOCK = 64
N_SELECT = 16
WINDOW = 512
CONV_WIDTH = 31
NEG_INF = -1e30
FORCE = 1e9
EPS = 1e-6

Q_W = N_HEADS * HEAD_DIM
KV_W = N_KV_HEADS * HEAD_DIM
NSA_GATE_W = N_HEADS * 3

MXU_DTYPE = jnp.bfloat16
LANES = 128
V7X_VMEM_BYTES = 64 * 1024 * 1024
VMEM_CAP = V7X_VMEM_BYTES - 6 * 1024 * 1024

F32 = jnp.float32


def _nbytes(shape, dtype):
    return int(np.prod(shape)) * jnp.dtype(dtype).itemsize


def _params(semantics, blocks, extra=0):
    need = 2 * sum(_nbytes(s, d) for s, d in blocks) + extra + (2 << 20)
    return pltpu.CompilerParams(dimension_semantics=semantics,
                                vmem_limit_bytes=int(min(max(need, 16 << 20), VMEM_CAP)))


def _tile(n, pref):
    return pref if n % pref == 0 else n


def _rmsnorm_kernel(x_ref, g_ref, o_ref):
    x = x_ref[...]
    ms = jnp.mean(x * x, axis=-1, keepdims=True)
    o_ref[...] = (x * lax.rsqrt(ms + EPS) * g_ref[...]).astype(o_ref.dtype)


def rmsnorm_cast(x, gain):
    m, d = x.shape
    tm = _tile(m, 256)
    return pl.pallas_call(
        _rmsnorm_kernel,
        grid=(m // tm,),
        in_specs=[pl.BlockSpec((tm, d), lambda i: (i, 0)),
                  pl.BlockSpec((1, d), lambda i: (0, 0))],
        out_specs=pl.BlockSpec((tm, d), lambda i: (i, 0)),
        out_shape=jax.ShapeDtypeStruct((m, d), MXU_DTYPE),
        compiler_params=_params(("parallel",), [((tm, d), F32), ((tm, d), MXU_DTYPE)],
                                extra=2 * _nbytes((tm, d), F32)),
        name="rmsnorm_cast",
    )(x, gain.reshape(1, d).astype(F32))


def _dual_kernel(x_ref, w1_ref, w2_ref, o_ref, *, act, n_blocks):
    j = pl.program_id(1)

    @pl.when(j < n_blocks)
    def _():
        x = x_ref[...]
        a = jnp.dot(x, w1_ref[...], preferred_element_type=F32)
        b = jnp.dot(x, w2_ref[...], preferred_element_type=F32)
        o_ref[...] = (act(a) * b).astype(o_ref.dtype)

    @pl.when(j >= n_blocks)
    def _():
        o_ref[...] = jnp.zeros_like(o_ref)


def dual_mm(x, w1, w2, act, out_dtype, name, tn=512, n_out=None):
    m, k = x.shape
    n = w1.shape[1]
    n_out = n if n_out is None else n_out
    tm, tn = _tile(m, 1024), _tile(n, tn)
    n_blocks = n // tn
    assert n_out % tn == 0
    wcol = lambda i, j: (0, jnp.minimum(j, n_blocks - 1))
    return pl.pallas_call(
        functools.partial(_dual_kernel, act=act, n_blocks=n_blocks),
        grid=(m // tm, n_out // tn),
        in_specs=[pl.BlockSpec((tm, k), lambda i, j: (i, 0)),
                  pl.BlockSpec((k, tn), wcol),
                  pl.BlockSpec((k, tn), wcol)],
        out_specs=pl.BlockSpec((tm, tn), lambda i, j: (i, j)),
        out_shape=jax.ShapeDtypeStruct((m, n_out), out_dtype),
        compiler_params=_params(("parallel", "arbitrary"),
                                [((tm, k), x.dtype), ((k, tn), w1.dtype), ((k, tn), w2.dtype),
                                 ((tm, tn), out_dtype)],
                                extra=4 * _nbytes((tm, tn), F32)),
        name=name,
    )(x, w1, w2)


def _mm_kernel(x_ref, w_ref, g_ref, o_ref, *, mode):
    acc = jnp.dot(x_ref[...], w_ref[...], preferred_element_type=F32)
    if mode == "headnorm":
        for c in range(acc.shape[1] // HEAD_DIM):
            sl = slice(c * HEAD_DIM, (c + 1) * HEAD_DIM)
            v = acc[:, sl]
            ms = jnp.mean(v * v, axis=-1, keepdims=True)
            o_ref[:, sl] = (v * lax.rsqrt(ms + EPS) * g_ref[:, sl]).astype(o_ref.dtype)
    elif mode == "sigmoid":
        o_ref[...] = jax.nn.sigmoid(acc).astype(o_ref.dtype)
    else:
        o_ref[...] = acc.astype(o_ref.dtype)


def mm(x, w, mode, out_dtype, name, gain=None):
    m, k = x.shape
    n = w.shape[1]
    tm, tn = _tile(m, 1024), _tile(n, 512)
    if gain is None:
        gain = jnp.ones((1, n), F32)
    return pl.pallas_call(
        functools.partial(_mm_kernel, mode=mode),
        grid=(m // tm, n // tn),
        in_specs=[pl.BlockSpec((tm, k), lambda i, j: (i, 0)),
                  pl.BlockSpec((k, tn), lambda i, j: (0, j)),
                  pl.BlockSpec((1, tn), lambda i, j: (0, j))],
        out_specs=pl.BlockSpec((tm, tn), lambda i, j: (i, j)),
        out_shape=jax.ShapeDtypeStruct((m, n), out_dtype),
        compiler_params=_params(("parallel", "arbitrary"),
                                [((tm, k), x.dtype), ((k, tn), w.dtype), ((tm, tn), out_dtype)],
                                extra=3 * _nbytes((tm, tn), F32)),
        name=name,
    )(x, w, gain.reshape(1, n).astype(F32))


def _mm_acc_kernel(h_ref, w_ref, r_ref, o_ref, acc_ref, *, scale, nk):
    kk = pl.program_id(2)

    @pl.when(kk == 0)
    def _():
        acc_ref[...] = jnp.zeros_like(acc_ref)

    acc_ref[...] += jnp.dot(h_ref[...], w_ref[...], preferred_element_type=F32)

    @pl.when(kk == nk - 1)
    def _():
        o_ref[...] = (r_ref[...] + scale * acc_ref[...]).astype(o_ref.dtype)


def mm_resid(h, w, resid, scale, tk, name):
    m, k = h.shape
    n = w.shape[1]
    tm, tn = _tile(m, 1024), _tile(n, 1024)
    tk = _tile(k, tk)
    nk = k // tk
    return pl.pallas_call(
        functools.partial(_mm_acc_kernel, scale=scale, nk=nk),
        grid=(m // tm, n // tn, nk),
        in_specs=[pl.BlockSpec((tm, tk), lambda i, j, kk: (i, kk)),
                  pl.BlockSpec((tk, tn), lambda i, j, kk: (kk, j)),
                  pl.BlockSpec((tm, tn), lambda i, j, kk: (i, j))],
        out_specs=pl.BlockSpec((tm, tn), lambda i, j, kk: (i, j)),
        out_shape=jax.ShapeDtypeStruct((m, n), F32),
        scratch_shapes=[pltpu.VMEM((tm, tn), F32)],
        compiler_params=_params(("parallel", "arbitrary", "arbitrary"),
                                [((tm, tk), h.dtype), ((tk, tn), w.dtype), ((tm, tn), F32),
                                 ((tm, tn), F32)],
                                extra=3 * _nbytes((tm, tn), F32)),
        name=name,
    )(h, w, resid)


def _gated_kernel(*refs, has_prev):
    if has_prev:
        a_ref, wo_ref, h_ref, wm_ref, p_ref, o_ref = refs
    else:
        a_ref, wo_ref, h_ref, wm_ref, o_ref = refs
    y = jnp.dot(a_ref[...], wo_ref[...], preferred_element_type=F32)
    g = jax.nn.sigmoid(jnp.dot(h_ref[...], wm_ref[...], preferred_element_type=F32))
    r = g * y
    if has_prev:
        r = p_ref[...] + r
    o_ref[...] = r.astype(o_ref.dtype)


def gated_mm(a, wo, h, wm, prev, out_dtype, name):
    m, ka = a.shape
    kh = h.shape[1]
    n = wo.shape[1]
    tm, tn = _tile(m, 512), _tile(n, 512)
    has_prev = prev is not None
    in_specs = [pl.BlockSpec((tm, ka), lambda i, j: (i, 0)),
                pl.BlockSpec((ka, tn), lambda i, j: (0, j)),
                pl.BlockSpec((tm, kh), lambda i, j: (i, 0)),
                pl.BlockSpec((kh, tn), lambda i, j: (0, j))]
    blocks = [((tm, ka), a.dtype), ((ka, tn), wo.dtype), ((tm, kh), h.dtype), ((kh, tn), wm.dtype),
              ((tm, tn), out_dtype)]
    args = [a, wo, h, wm]
    if has_prev:
        in_specs.append(pl.BlockSpec((tm, tn), lambda i, j: (i, j)))
        blocks.append(((tm, tn), prev.dtype))
        args.append(prev)
    return pl.pallas_call(
        functools.partial(_gated_kernel, has_prev=has_prev),
        grid=(m // tm, n // tn),
        in_specs=in_specs,
        out_specs=pl.BlockSpec((tm, tn), lambda i, j: (i, j)),
        out_shape=jax.ShapeDtypeStruct((m, n), out_dtype),
        compiler_params=_params(("parallel", "arbitrary"), blocks, extra=4 * _nbytes((tm, tn), F32)),
        name=name,
    )(*args)


def _gelu_tanh(x):
    return 0.5 * x * (1.0 + jnp.tanh(np.sqrt(2.0 / np.pi).astype(np.float32) * (x + 0.044715 * (x * x * x))))


def _compress_kernel(a_ref, pa_ref, pb_ref, w1a_ref, w1b_ref, w2_ref, g_ref, o_ref, *, norm):
    a = a_ref[0]
    nch = a.shape[0]
    ua = jnp.dot((a + pa_ref[...]).astype(MXU_DTYPE), w1a_ref[...], preferred_element_type=F32)
    ub = jnp.dot((a + pb_ref[...]).astype(MXU_DTYPE), w1b_ref[...], preferred_element_type=F32)
    pre = ua + pltpu.roll(ub, nch - 1, 0)
    hid = _gelu_tanh(pre).astype(MXU_DTYPE)
    out = jnp.dot(hid, w2_ref[...], preferred_element_type=F32)
    if norm:
        ms = jnp.mean(out * out, axis=-1, keepdims=True)
        out = out * lax.rsqrt(ms + EPS) * g_ref[...]
    o_ref[0] = out.astype(o_ref.dtype)


def compress(chunks, pos, w1, w2, gain, norm, name):
    hkv, nch, cw = chunks.shape
    hid = w1.shape[1]
    posf = pos.reshape(1, CMP_BLOCK * HEAD_DIM).astype(F32)
    pa, pb = posf[:, :cw], posf[:, cw:]
    w1c = w1.astype(MXU_DTYPE)
    w1a, w1b = w1c[:cw], w1c[cw:]
    full = lambda shape: pl.BlockSpec(shape, lambda h: (0,) * len(shape))
    return pl.pallas_call(
        functools.partial(_compress_kernel, norm=norm),
        grid=(hkv,),
        in_specs=[pl.BlockSpec((1, nch, cw), lambda h: (h, 0, 0)),
                  full((1, cw)), full((1, cw)), full((cw, hid)), full((cw, hid)),
                  full((hid, HEAD_DIM)), full((1, HEAD_DIM))],
        out_specs=pl.BlockSpec((1, nch, HEAD_DIM), lambda h: (h, 0, 0)),
        out_shape=jax.ShapeDtypeStruct((hkv, nch, HEAD_DIM), MXU_DTYPE),
        compiler_params=_params(("parallel",),
                                [((nch, cw), F32), ((cw, hid), MXU_DTYPE), ((cw, hid), MXU_DTYPE)],
                                extra=8 * _nbytes((nch, cw), F32)),
        name=name,
    )(chunks, pa, pb, w1a, w1b, w2.astype(MXU_DTYPE), gain.reshape(1, HEAD_DIM).astype(F32))


def _conv_kernel(cur_ref, prev_ref, cw_ref, cb_ref, lg_ref, lb_ref, o_ref, ext_ref, y_ref, *, halo, rc, cc):
    i = pl.program_id(0)
    tm, ch = cur_ref.shape
    ext_ref[0:halo, :] = jnp.where(i > 0, prev_ref[tm - halo:tm, :], 0.0)
    ext_ref[halo:halo + tm, :] = cur_ref[...]
    base = halo - (CONV_WIDTH - 1)
    for c0 in range(0, ch, cc):
        for r0 in range(0, tm, rc):
            acc = jnp.zeros((rc, cc), F32) + cb_ref[:, c0:c0 + cc]
            for w in range(CONV_WIDTH):
                acc = acc + ext_ref[base + r0 + w:base + r0 + w + rc, c0:c0 + cc] * cw_ref[w:w + 1, c0:c0 + cc]
            y_ref[r0:r0 + rc, c0:c0 + cc] = acc
    y = y_ref[...]
    mu = jnp.mean(y, axis=-1, keepdims=True)
    d = y - mu
    var = jnp.mean(d * d, axis=-1, keepdims=True)
    z = d * lax.rsqrt(var + EPS) * lg_ref[...] + lb_ref[...]
    o_ref[...] = (z * jax.nn.sigmoid(z)).astype(o_ref.dtype)


def conv_module(u, conv_w, conv_b, ln_g, ln_b):
    t, ch = u.shape
    tm = _tile(t, 256)
    halo = 32
    row = lambda v: v.reshape(1, ch).astype(F32)
    full = lambda shape: pl.BlockSpec(shape, lambda i: (0, 0))
    return pl.pallas_call(
        functools.partial(_conv_kernel, halo=halo, rc=64, cc=min(512, ch)),
        grid=(t // tm,),
        in_specs=[pl.BlockSpec((tm, ch), lambda i: (i, 0)),
                  pl.BlockSpec((tm, ch), lambda i: (jnp.maximum(i - 1, 0), 0)),
                  full((CONV_WIDTH, ch)), full((1, ch)), full((1, ch)), full((1, ch))],
        out_specs=pl.BlockSpec((tm, ch), lambda i: (i, 0)),
        out_shape=jax.ShapeDtypeStruct((t, ch), MXU_DTYPE),
        scratch_shapes=[pltpu.VMEM((tm + halo, ch), F32), pltpu.VMEM((tm, ch), F32)],
        compiler_params=_params(("parallel",),
                                [((tm, ch), F32), ((tm, ch), F32), ((tm, ch), MXU_DTYPE)],
                                extra=6 * _nbytes((tm, ch), F32)),
        name="conv_module",
    )(u, u, conv_w.astype(F32), row(conv_b), row(ln_g), row(ln_b))


def _dot_t(a, b):
    return lax.dot_general(a, b, (((0,), (0,)), ((), ())), preferred_element_type=F32)


def _sort_desc(c):
    n = len(c)
    if n == 1:
        return list(c)
    return _merge_desc(_sort_desc(c[:n // 2]) + _sort_desc(c[n // 2:])[::-1])


def _merge_desc(c):
    c = list(c)
    n = len(c)
    d = n // 2
    while d >= 1:
        for a in range(n):
            if a & d == 0:
                c[a], c[a + d] = jnp.maximum(c[a], c[a + d]), jnp.minimum(c[a], c[a + d])
        d //= 2
    return c


def _kth_largest(score, k):
    ns = score.shape[0]
    sub = 8
    a = _sort_desc([score[j * sub:(j + 1) * sub, :] for j in range(ns // sub)])[:k]
    shift = sub // 2
    while True:
        b = [pltpu.roll(v, shift, 0) for v in a]
        if 2 * len(a) <= k:
            top = a + b[::-1]
        else:
            top = [jnp.maximum(a[j], b[len(a) - 1 - j]) for j in range(len(a))]
        if shift == 1:
            break
        a = _merge_desc(top)
        shift //= 2
    kth = top[0]
    for v in top[1:]:
        kth = jnp.minimum(kth, v)
    return kth[0:1, :]


def _nsa_kernel(q_ref, kc_ref, vc_ref, ks_ref, kw_ref, vs_ref, vw_ref, gt_ref, mt_ref, lt_ref, e_ref, o_ref,
                qx_ref, biasx_ref, m_ref, l_ref, acc_ref, out_ref, *, tq, n_top, wl):
    i = pl.program_id(1)
    t0 = i * tq
    nc = kc_ref.shape[1]
    ns = mt_ref.shape[0]
    nb = tq // SEL_BLOCK
    gsl = [slice(g * tq, (g + 1) * tq) for g in range(GROUP)]

    def gate(g, b):
        return gt_ref[0, g * 3 + b:g * 3 + b + 1, :]

    def staggered(scores, consume, ahead):
        pending = [scores(g) for g in range(min(ahead, GROUP))]
        for g in range(GROUP):
            if g + ahead < GROUP:
                pending.append(scores(g + ahead))
            consume(g, pending[g])

    for g in range(GROUP):
        qg = q_ref[:, g * HEAD_DIM:(g + 1) * HEAD_DIM].astype(F32)
        qx_ref[0:HEAD_DIM, gsl[g]] = qg.T.astype(MXU_DTYPE)
    qx_ref[HEAD_DIM:, :] = jnp.zeros((HEAD_DIM, GROUP * tq), MXU_DTYPE)
    t_q = t0 + lax.broadcasted_iota(jnp.int32, (1, tq), 1)

    cmp_end = lax.broadcasted_iota(jnp.int32, (nc, 1), 0) * CMP_STRIDE + (CMP_BLOCK - 1)
    cbias = jnp.where(cmp_end <= t_q, 0.0, NEG_INF)
    kc = kc_ref[0]
    vc = vc_ref[0]
    psums = []

    def cmp_scores(g):
        return jnp.dot(kc, qx_ref[0:HEAD_DIM, gsl[g]], preferred_element_type=F32) + cbias

    def cmp_consume(g, sm):
        mx = jnp.maximum(jnp.max(sm, axis=0, keepdims=True), 0.5 * NEG_INF)
        e = jnp.exp2(sm - mx)
        den = jnp.sum(e, axis=0, keepdims=True)
        p = e * (1.0 / jnp.where(den > 0.0, den, 1.0))
        out_ref[:, gsl[g]] = gate(g, 0) * _dot_t(vc, p.astype(MXU_DTYPE))
        psums.append(p)

    staggered(cmp_scores, cmp_consume, 1)
    psum = functools.reduce(lambda a, b: a + b, psums)

    p_hi = psum.astype(MXU_DTYPE)
    p_lo = (psum - p_hi.astype(F32)).astype(MXU_DTYPE)
    imp = (jnp.dot(mt_ref[...], p_hi, preferred_element_type=F32)
           + jnp.dot(mt_ref[...], p_lo, preferred_element_type=F32))
    s_id = lax.broadcasted_iota(jnp.int32, (ns, 1), 0)
    cur = t_q // SEL_BLOCK
    forced = (s_id == 0) | (s_id == cur) | (s_id == cur - 1)
    valid = s_id * SEL_BLOCK <= t_q
    score = jnp.where(forced, FORCE, jnp.where(valid, imp, NEG_INF))
    kth = _kth_largest(score, n_top)
    above = score > kth
    equal = score == kth
    n_above = jnp.sum(jnp.where(above, 1.0, 0.0), axis=0, keepdims=True)
    n_equal_upto = jnp.dot(lt_ref[...], jnp.where(equal, 1.0, 0.0).astype(MXU_DTYPE),
                           preferred_element_type=F32)
    picked = above | (equal & (n_equal_upto <= n_top - n_above))
    block_bias = jnp.where(picked, 0.0, NEG_INF)
    pad_rows = jnp.zeros((biasx_ref.shape[1] - nb, tq), F32)
    for j in range(ns // nb):
        slab = jnp.concatenate([block_bias[j * nb:(j + 1) * nb, :], pad_rows], axis=0)
        biasx_ref[j] = jnp.concatenate([slab] * GROUP, axis=1).astype(MXU_DTYPE)

    ws = pl.multiple_of(jnp.maximum(t0 + tq - wl, 0), SEL_BLOCK)
    back = t_q - (ws + lax.broadcasted_iota(jnp.int32, (wl, 1), 0))
    wbias = jnp.where(lax.bitcast_convert_type(back, jnp.uint32) < WINDOW, 0.0, NEG_INF)
    kw = kw_ref[pl.ds(ws, wl), :]
    vw = vw_ref[pl.ds(ws, wl), :]

    def win_scores(g):
        return jnp.dot(kw, qx_ref[0:HEAD_DIM, gsl[g]], preferred_element_type=F32) + wbias

    def win_consume(g, sm):
        e = jnp.exp2(sm - jnp.max(sm, axis=0, keepdims=True))
        inv = 1.0 / jnp.sum(e, axis=0, keepdims=True)
        out_ref[:, gsl[g]] += (gate(g, 2) * inv) * _dot_t(vw, e.astype(MXU_DTYPE))

    staggered(win_scores, win_consume, 1)

    m_ref[...] = jnp.full(m_ref.shape, NEG_INF, F32)
    l_ref[...] = jnp.zeros(l_ref.shape, F32)
    acc_ref[...] = jnp.zeros(acc_ref.shape, F32)

    def tile(kt, diagonal):
        k0 = pl.multiple_of(kt * tq, tq)
        kx = jnp.concatenate([ks_ref[pl.ds(k0, tq), :], e_ref[...]], axis=1)
        vv = vs_ref[pl.ds(k0, tq), :]
        qx_ref[HEAD_DIM:HEAD_DIM + biasx_ref.shape[1], :] = biasx_ref[kt]
        if diagonal:
            causal = k0 + lax.broadcasted_iota(jnp.int32, (tq, 1), 0) <= t_q

        def scores(g):
            sm = jnp.dot(kx, qx_ref[:, gsl[g]], preferred_element_type=F32)
            return jnp.where(causal, sm, NEG_INF) if diagonal else sm

        def consume(g, sm):
            m_old = m_ref[:, gsl[g]]
            m_new = jnp.maximum(m_old, jnp.max(sm, axis=0, keepdims=True))
            alpha = jnp.exp2(m_old - m_new)
            pt = jnp.exp2(sm - m_new)
            l_ref[:, gsl[g]] = alpha * l_ref[:, gsl[g]] + jnp.sum(pt, axis=0, keepdims=True)
            acc_ref[:, gsl[g]] = alpha * acc_ref[:, gsl[g]] + _dot_t(vv, pt.astype(MXU_DTYPE))
            m_ref[:, gsl[g]] = m_new

        staggered(scores, consume, GROUP)

    def interior(kt, carry):
        tile(kt, False)
        return carry

    lax.fori_loop(0, i, interior, 0)
    tile(i, True)

    for g in range(GROUP):
        out_t = out_ref[:, gsl[g]] + (gate(g, 1) / l_ref[:, gsl[g]]) * acc_ref[:, gsl[g]]
        o_ref[:, g * HEAD_DIM:(g + 1) * HEAD_DIM] = out_t.T.astype(o_ref.dtype)


def nsa_attention(q, k_cmp, v_cmp, k_sw, v_sw, gates_t, tq=256):
    t = q.shape[0]
    tq = min(tq, t)
    wl = min(WINDOW + tq, t)
    nc = k_cmp.shape[1]
    ns = t // SEL_BLOCK
    n_top = min(N_SELECT, ns)
    r = GROUP * tq
    lt = jnp.asarray(np.tril(np.ones((ns, ns))), MXU_DTYPE)
    cj = np.arange(nc)[None, :] * CMP_STRIDE
    si = np.arange(ns)[:, None] * SEL_BLOCK
    ov = np.clip(np.minimum(cj + CMP_BLOCK, si + SEL_BLOCK) - np.maximum(cj, si), 0, None) / CMP_BLOCK
    ov[:, (t - CMP_BLOCK) // CMP_STRIDE + 1:] = 0.0
    mt = jnp.asarray(ov, MXU_DTYPE)
    nb = tq // SEL_BLOCK
    onehot = jnp.asarray(np.arange(tq)[:, None] // SEL_BLOCK == np.arange(HEAD_DIM)[None, :], MXU_DTYPE)
    bias_rows = 16
    assert nb <= bias_rows and ns % nb == 0

    head = lambda off: pl.BlockSpec((t, HEAD_DIM), lambda h, i: (0, off + h))
    return pl.pallas_call(
        functools.partial(_nsa_kernel, tq=tq, n_top=n_top, wl=wl),
        grid=(N_KV_HEADS, t // tq),
        in_specs=[pl.BlockSpec((tq, GROUP * HEAD_DIM), lambda h, i: (i, h)),
                  pl.BlockSpec((1, nc, HEAD_DIM), lambda h, i: (h, 0, 0)),
                  pl.BlockSpec((1, nc, HEAD_DIM), lambda h, i: (h, 0, 0)),
                  head(0), head(N_KV_HEADS), head(0), head(N_KV_HEADS),
                  pl.BlockSpec((1, 16, tq), lambda h, i: (h, 0, i)),
                  pl.BlockSpec((ns, nc), lambda h, i: (0, 0)),
                  pl.BlockSpec((ns, ns), lambda h, i: (0, 0)),
                  pl.BlockSpec((tq, HEAD_DIM), lambda h, i: (0, 0))],
        out_specs=pl.BlockSpec((tq, GROUP * HEAD_DIM), lambda h, i: (i, h)),
        out_shape=jax.ShapeDtypeStruct((t, Q_W), MXU_DTYPE),
        scratch_shapes=[pltpu.VMEM((2 * HEAD_DIM, r), MXU_DTYPE),
                        pltpu.VMEM((ns // nb, bias_rows, r), MXU_DTYPE),
                        pltpu.VMEM((1, r), F32), pltpu.VMEM((1, r), F32),
                        pltpu.VMEM((HEAD_DIM, r), F32), pltpu.VMEM((HEAD_DIM, r), F32)],
        compiler_params=_params(("parallel", "arbitrary"),
                                [((t, HEAD_DIM), MXU_DTYPE)] * 4 + [((nc, HEAD_DIM), MXU_DTYPE)] * 2
                                + [((tq, GROUP * HEAD_DIM), MXU_DTYPE)] * 2 + [((ns, nc), MXU_DTYPE)],
                                extra=12 * _nbytes((max(nc, wl), tq), F32)),
        name="nsa_attention",
    )(q, k_cmp, v_cmp, k_sw, k_sw, v_sw, v_sw, gates_t, mt, lt, onehot)


def _pad_cols(w, n):
    return jnp.pad(w, ((0, 0), (0, n - w.shape[1])))


def _ffn(x, norm_g, w_gate, w_up, w_down):
    f = w_gate.shape[1]
    fp = -(-f // 1024) * 1024 if f > 1024 else f
    wd = jnp.pad(w_down.astype(MXU_DTYPE), ((0, fp - f), (0, 0)))
    xn = rmsnorm_cast(x, norm_g)
    h = dual_mm(xn, w_gate.astype(MXU_DTYPE), w_up.astype(MXU_DTYPE), jax.nn.silu, MXU_DTYPE, "ffn_gate_up",
                tn=256, n_out=fp)
    tk = fp // 4 if (fp // 4) % LANES == 0 else fp
    return mm_resid(h, wd, x, 0.5, tk, "ffn_down")


def _layer(x, p):
    t, d = x.shape
    x = _ffn(x, p["ffn1_norm"], p["ffn1_w_gate"], p["ffn1_w_up"], p["ffn1_w_down"])

    hn = rmsnorm_cast(x, p["mix_norm"])
    w_in = p["w_in"].astype(MXU_DTYPE)
    conv_ch = p["conv_w"].shape[1]
    o = 0
    wq = w_in[:, o:o + Q_W]; o += Q_W
    wkc = w_in[:, o:o + KV_W]; o += KV_W
    wvc = w_in[:, o:o + KV_W]; o += KV_W
    wks = w_in[:, o:o + KV_W]; o += KV_W
    wvs = w_in[:, o:o + KV_W]; o += KV_W
    wkw = w_in[:, o:o + KV_W]; o += KV_W
    wvw = w_in[:, o:o + KV_W]; o += KV_W
    wgn = w_in[:, o:o + NSA_GATE_W]; o += NSA_GATE_W
    wga = w_in[:, o:o + conv_ch]; o += conv_ch
    wgg = w_in[:, o:o + conv_ch]; o += conv_ch
    wm0 = w_in[:, o:o + d]; o += d
    wm1 = w_in[:, o:o + d]; o += d

    scale = HEAD_DIM ** -0.5 * np.log2(np.e)
    q = mm(hn, wq, "headnorm", MXU_DTYPE, "proj_q", gain=jnp.tile(p["q_norm"] * scale, N_HEADS))
    k_sw = mm(hn, jnp.concatenate([wks, wkw], axis=1), "headnorm", MXU_DTYPE, "proj_k",
              gain=jnp.concatenate([jnp.tile(p["k_norm"][1], N_KV_HEADS), jnp.tile(p["k_norm"][2], N_KV_HEADS)]))
    v_sw = mm(hn, jnp.concatenate([wvs, wvw], axis=1), "plain", MXU_DTYPE, "proj_v")
    kv_c = mm(hn, jnp.concatenate([wkc, wvc], axis=1), "plain", F32, "proj_kv_cmp")
    g_nsa = mm(hn, _pad_cols(wgn, LANES), "sigmoid", F32, "proj_gates")
    u = dual_mm(hn, wgg, wga, jax.nn.sigmoid, F32, "proj_glu")

    chunks = kv_c.reshape(t // CMP_STRIDE, CMP_STRIDE, 2, N_KV_HEADS, HEAD_DIM)
    chunks = chunks.transpose(2, 3, 0, 1, 4).reshape(2, N_KV_HEADS, t // CMP_STRIDE, CMP_STRIDE * HEAD_DIM)
    k_cmp = compress(chunks[0], p["cmp_pos_k"], p["cmp_k_w1"], p["cmp_k_w2"], p["k_norm"][0], True, "compress_k")
    v_cmp = compress(chunks[1], p["cmp_pos_v"], p["cmp_v_w1"], p["cmp_v_w2"], p["k_norm"][0], False, "compress_v")

    gates_t = g_nsa[:, :NSA_GATE_W].reshape(t, N_KV_HEADS, GROUP * 3).transpose(1, 2, 0)
    gates_t = jnp.pad(gates_t, ((0, 0), (0, 16 - GROUP * 3), (0, 0)))
    attn = nsa_attention(q, k_cmp, v_cmp, k_sw, v_sw, gates_t)

    uc = conv_module(u, p["conv_w"], p["conv_b"], p["conv_ln_g"], p["conv_ln_b"])

    mix = gated_mm(attn, p["nsa_w_o"].astype(MXU_DTYPE), hn, wm0, None, F32, "merge_nsa")
    mix = gated_mm(uc, p["conv_w_o"].astype(MXU_DTYPE), hn, wm1, mix, MXU_DTYPE, "merge_conv")
    x = mm_resid(mix, p["w_out"].astype(MXU_DTYPE), x, 1.0, d, "mix_out")

    return _ffn(x, p["ffn2_norm"], p["ffn2_w_gate"], p["ffn2_w_up"], p["ffn2_w_down"])


_NAMES = ("ffn1_norm", "ffn1_w_gate", "ffn1_w_up", "ffn1_w_down", "mix_norm", "w_in", "q_norm", "k_norm",
          "cmp_pos_k", "cmp_k_w1", "cmp_k_w2", "cmp_pos_v", "cmp_v_w1", "cmp_v_w2", "nsa_w_o",
          "conv_w", "conv_b", "conv_ln_g", "conv_ln_b", "conv_w_o", "w_out",
          "ffn2_norm", "ffn2_w_gate", "ffn2_w_up", "ffn2_w_down")


def kernel(x, ffn1_norm, ffn1_w_gate, ffn1_w_up, ffn1_w_down, mix_norm, w_in, q_norm, k_norm, cmp_pos_k, cmp_k_w1, cmp_k_w2, cmp_pos_v, cmp_v_w1, cmp_v_w2, nsa_w_o, conv_w, conv_b, conv_ln_g, conv_ln_b, conv_w_o, w_out, ffn2_norm, ffn2_w_gate, ffn2_w_up, ffn2_w_down):
    params = (ffn1_norm, ffn1_w_gate, ffn1_w_up, ffn1_w_down, mix_norm, w_in, q_norm, k_norm, cmp_pos_k,
              cmp_k_w1, cmp_k_w2, cmp_pos_v, cmp_v_w1, cmp_v_w2, nsa_w_o, conv_w, conv_b, conv_ln_g,
              conv_ln_b, conv_w_o, w_out, ffn2_norm, ffn2_w_gate, ffn2_w_up, ffn2_w_down)
    b, t, d = x.shape
    depth = ffn1_norm.shape[0]
    xs = x.reshape(b * t, d)
    outs = []
    for bi in range(b):
        xb = xs[bi * t:(bi + 1) * t]
        for l in range(depth):
            xb = _layer(xb, {n: a[l] for n, a in zip(_NAMES, params)})
        outs.append(xb)
    out = outs[0] if b == 1 else jnp.concatenate(outs, axis=0)
    return out.reshape(b, t, d)
```

```python
import functools

import numpy as np
import jax
import jax.numpy as jnp
from jax import lax
from jax.experimental import pallas as pl
from jax.experimental.pallas import tpu as pltpu

HEAD_DIM = 128
N_HEADS = 16
N_KV_HEADS = 4
GROUP = N_HEADS // N_KV_HEADS
CMP_BLOCK = 32
CMP_STRIDE = 16
SEL_BLOCK = 64
N_SELECT = 16
WINDOW = 512
CONV_WIDTH = 31
NEG_INF = -1e30
FORCE = 1e9
EPS = 1e-6

Q_W = N_HEADS * HEAD_DIM
KV_W = N_KV_HEADS * HEAD_DIM
NSA_GATE_W = N_HEADS * 3

MXU_DTYPE = jnp.bfloat16
LANES = 128
V7X_VMEM_BYTES = 64 * 1024 * 1024
VMEM_CAP = V7X_VMEM_BYTES - 6 * 1024 * 1024

F32 = jnp.float32


def _nbytes(shape, dtype):
    return int(np.prod(shape)) * jnp.dtype(dtype).itemsize


def _params(semantics, blocks, extra=0):
    need = 2 * sum(_nbytes(s, d) for s, d in blocks) + extra + (2 << 20)
    return pltpu.CompilerParams(dimension_semantics=semantics,
                                vmem_limit_bytes=int(min(max(need, 16 << 20), VMEM_CAP)))


def _tile(n, pref):
    return pref if n % pref == 0 else n


def _rmsnorm_kernel(x_ref, g_ref, o_ref):
    x = x_ref[...]
    ms = jnp.mean(x * x, axis=-1, keepdims=True)
    o_ref[...] = (x * lax.rsqrt(ms + EPS) * g_ref[...]).astype(o_ref.dtype)


def rmsnorm_cast(x, gain):
    m, d = x.shape
    tm = _tile(m, 256)
    return pl.pallas_call(
        _rmsnorm_kernel,
        grid=(m // tm,),
        in_specs=[pl.BlockSpec((tm, d), lambda i: (i, 0)),
                  pl.BlockSpec((1, d), lambda i: (0, 0))],
        out_specs=pl.BlockSpec((tm, d), lambda i: (i, 0)),
        out_shape=jax.ShapeDtypeStruct((m, d), MXU_DTYPE),
        compiler_params=_params(("parallel",), [((tm, d), F32), ((tm, d), MXU_DTYPE)],
                                extra=2 * _nbytes((tm, d), F32)),
        name="rmsnorm_cast",
    )(x, gain.reshape(1, d).astype(F32))


def _dual_kernel(x_ref, w1_ref, w2_ref, o_ref, *, act, n_blocks):
    j = pl.program_id(1)

    @pl.when(j < n_blocks)
    def _():
        x = x_ref[...]
        a = jnp.dot(x, w1_ref[...].astype(x.dtype), preferred_element_type=F32)
        b = jnp.dot(x, w2_ref[...].astype(x.dtype), preferred_element_type=F32)
        o_ref[...] = (act(a) * b).astype(o_ref.dtype)

    @pl.when(j >= n_blocks)
    def _():
        o_ref[...] = jnp.zeros_like(o_ref)


def dual_mm(x, w1, w2, act, out_dtype, name, tn=512, n_out=None):
    m, k = x.shape
    n = w1.shape[1]
    n_out = n if n_out is None else n_out
    tm, tn = _tile(m, 1024), _tile(n, tn)
    n_blocks = n // tn
    assert n_out % tn == 0
    wcol = lambda i, j: (0, jnp.minimum(j, n_blocks - 1))
    return pl.pallas_call(
        functools.partial(_dual_kernel, act=act, n_blocks=n_blocks),
        grid=(m // tm, n_out // tn),
        in_specs=[pl.BlockSpec((tm, k), lambda i, j: (i, 0)),
                  pl.BlockSpec((k, tn), wcol),
                  pl.BlockSpec((k, tn), wcol)],
        out_specs=pl.BlockSpec((tm, tn), lambda i, j: (i, j)),
        out_shape=jax.ShapeDtypeStruct((m, n_out), out_dtype),
        compiler_params=_params(("parallel", "arbitrary"),
                                [((tm, k), x.dtype), ((k, tn), w1.dtype), ((k, tn), w2.dtype),
                                 ((tm, tn), out_dtype)],
                                extra=4 * _nbytes((tm, tn), F32)),
        name=name,
    )(x, w1, w2)


def _mm_kernel(x_ref, w_ref, g_ref, o_ref, *, mode):
    x = x_ref[...]
    acc = jnp.dot(x, w_ref[...].astype(x.dtype), preferred_element_type=F32)
    if mode == "headnorm":
        for c in range(acc.shape[1] // HEAD_DIM):
            sl = slice(c * HEAD_DIM, (c + 1) * HEAD_DIM)
            v = acc[:, sl]
            ms = jnp.mean(v * v, axis=-1, keepdims=True)
            o_ref[:, sl] = (v * lax.rsqrt(ms + EPS) * g_ref[:, sl]).astype(o_ref.dtype)
    elif mode == "sigmoid":
        o_ref[...] = jax.nn.sigmoid(acc).astype(o_ref.dtype)
    else:
        o_ref[...] = acc.astype(o_ref.dtype)


def mm(x, w, mode, out_dtype, name, gain=None, n=None, wcol=None):
    m, k = x.shape
    n = w.shape[1] if n is None else n
    tm, tn = _tile(m, 1024), _tile(n, 512)
    if gain is None:
        gain = jnp.ones((1, n), F32)
    if wcol is None:
        wcol = lambda j: j
    return pl.pallas_call(
        functools.partial(_mm_kernel, mode=mode),
        grid=(m // tm, n // tn),
        in_specs=[pl.BlockSpec((tm, k), lambda i, j: (i, 0)),
                  pl.BlockSpec((k, tn), lambda i, j: (0, wcol(j))),
                  pl.BlockSpec((1, tn), lambda i, j: (0, j))],
        out_specs=pl.BlockSpec((tm, tn), lambda i, j: (i, j)),
        out_shape=jax.ShapeDtypeStruct((m, n), out_dtype),
        compiler_params=_params(("parallel", "arbitrary"),
                                [((tm, k), x.dtype), ((k, tn), w.dtype), ((tm, tn), out_dtype)],
                                extra=3 * _nbytes((tm, tn), F32)),
        name=name,
    )(x, w, gain.reshape(1, n).astype(F32))


def _mm_acc_kernel(h_ref, w_ref, r_ref, o_ref, acc_ref, *, scale, nk):
    kk = pl.program_id(2)

    @pl.when(kk == 0)
    def _():
        acc_ref[...] = jnp.zeros_like(acc_ref)

    h = h_ref[...]
    acc_ref[...] += jnp.dot(h, w_ref[...].astype(h.dtype), preferred_element_type=F32)

    @pl.when(kk == nk - 1)
    def _():
        o_ref[...] = (r_ref[...] + scale * acc_ref[...]).astype(o_ref.dtype)


def mm_resid(h, w, resid, scale, tk, name, tn=1024):
    m, k = h.shape
    n = w.shape[1]
    tm, tn = _tile(m, 1024), _tile(n, tn)
    tk = _tile(k, tk)
    nk = k // tk
    return pl.pallas_call(
        functools.partial(_mm_acc_kernel, scale=scale, nk=nk),
        grid=(m // tm, n // tn, nk),
        in_specs=[pl.BlockSpec((tm, tk), lambda i, j, kk: (i, kk)),
                  pl.BlockSpec((tk, tn), lambda i, j, kk: (kk, j)),
                  pl.BlockSpec((tm, tn), lambda i, j, kk: (i, j))],
        out_specs=pl.BlockSpec((tm, tn), lambda i, j, kk: (i, j)),
        out_shape=jax.ShapeDtypeStruct((m, n), F32),
        scratch_shapes=[pltpu.VMEM((tm, tn), F32)],
        compiler_params=_params(("parallel", "arbitrary", "arbitrary"),
                                [((tm, tk), h.dtype), ((tk, tn), w.dtype), ((tm, tn), F32),
                                 ((tm, tn), F32)],
                                extra=3 * _nbytes((tm, tn), F32)),
        name=name,
    )(h, w, resid)


def _gated_kernel(*refs, has_prev):
    if has_prev:
        a_ref, wo_ref, h_ref, wm_ref, p_ref, o_ref = refs
    else:
        a_ref, wo_ref, h_ref, wm_ref, o_ref = refs
    a = a_ref[...]
    y = jnp.dot(a, wo_ref[...].astype(a.dtype), preferred_element_type=F32)
    g = jax.nn.sigmoid(jnp.dot(h_ref[...], wm_ref[...], preferred_element_type=F32))
    r = g * y
    if has_prev:
        r = p_ref[...] + r
    o_ref[...] = r.astype(o_ref.dtype)


def gated_mm(a, wo, h, wm, prev, out_dtype, name):
    m, ka = a.shape
    kh = h.shape[1]
    n = wo.shape[1]
    tm, tn = _tile(m, 512), _tile(n, 512)
    has_prev = prev is not None
    in_specs = [pl.BlockSpec((tm, ka), lambda i, j: (i, 0)),
                pl.BlockSpec((ka, tn), lambda i, j: (0, j)),
                pl.BlockSpec((tm, kh), lambda i, j: (i, 0)),
                pl.BlockSpec((kh, tn), lambda i, j: (0, j))]
    blocks = [((tm, ka), a.dtype), ((ka, tn), wo.dtype), ((tm, kh), h.dtype), ((kh, tn), wm.dtype),
              ((tm, tn), out_dtype)]
    args = [a, wo, h, wm]
    if has_prev:
        in_specs.append(pl.BlockSpec((tm, tn), lambda i, j: (i, j)))
        blocks.append(((tm, tn), prev.dtype))
        args.append(prev)
    return pl.pallas_call(
        functools.partial(_gated_kernel, has_prev=has_prev),
        grid=(m // tm, n // tn),
        in_specs=in_specs,
        out_specs=pl.BlockSpec((tm, tn), lambda i, j: (i, j)),
        out_shape=jax.ShapeDtypeStruct((m, n), out_dtype),
        compiler_params=_params(("parallel", "arbitrary"), blocks, extra=4 * _nbytes((tm, tn), F32)),
        name=name,
    )(*args)


def _gelu_tanh(x):
    return 0.5 * x * (1.0 + jnp.tanh(np.sqrt(2.0 / np.pi).astype(np.float32) * (x + 0.044715 * (x * x * x))))


def _compress_kernel(a_ref, pa_ref, pb_ref, w1a_ref, w1b_ref, w2_ref, g_ref, o_ref, *, norm):
    a = a_ref[0]
    nch = a.shape[0]
    ua = jnp.dot((a + pa_ref[...]).astype(MXU_DTYPE), w1a_ref[...], preferred_element_type=F32)
    ub = jnp.dot((a + pb_ref[...]).astype(MXU_DTYPE), w1b_ref[...], preferred_element_type=F32)
    pre = ua + pltpu.roll(ub, nch - 1, 0)
    hid = _gelu_tanh(pre).astype(MXU_DTYPE)
    out = jnp.dot(hid, w2_ref[...], preferred_element_type=F32)
    if norm:
        ms = jnp.mean(out * out, axis=-1, keepdims=True)
        out = out * lax.rsqrt(ms + EPS) * g_ref[...]
    o_ref[0] = out.astype(o_ref.dtype)


def compress(chunks, pos, w1, w2, gain, norm, name):
    hkv, nch, cw = chunks.shape
    hid = w1.shape[1]
    posf = pos.reshape(1, CMP_BLOCK * HEAD_DIM).astype(F32)
    pa, pb = posf[:, :cw], posf[:, cw:]
    w1c = w1.astype(MXU_DTYPE)
    w1a, w1b = w1c[:cw], w1c[cw:]
    full = lambda shape: pl.BlockSpec(shape, lambda h: (0,) * len(shape))
    return pl.pallas_call(
        functools.partial(_compress_kernel, norm=norm),
        grid=(hkv,),
        in_specs=[pl.BlockSpec((1, nch, cw), lambda h: (h, 0, 0)),
                  full((1, cw)), full((1, cw)), full((cw, hid)), full((cw, hid)),
                  full((hid, HEAD_DIM)), full((1, HEAD_DIM))],
        out_specs=pl.BlockSpec((1, nch, HEAD_DIM), lambda h: (h, 0, 0)),
        out_shape=jax.ShapeDtypeStruct((hkv, nch, HEAD_DIM), MXU_DTYPE),
        compiler_params=_params(("parallel",),
                                [((nch, cw), F32), ((cw, hid), MXU_DTYPE), ((cw, hid), MXU_DTYPE)],
                                extra=8 * _nbytes((nch, cw), F32)),
        name=name,
    )(chunks, pa, pb, w1a, w1b, w2.astype(MXU_DTYPE), gain.reshape(1, HEAD_DIM).astype(F32))


def _conv_kernel(cur_ref, prev_ref, cw_ref, cb_ref, lg_ref, lb_ref, o_ref, ext_ref, xs_ref, y_ref, *,
                 halo, rc, cc):
    i = pl.program_id(0)
    tm, ch = cur_ref.shape
    sub = 8
    ext_ref[0:halo, :] = jnp.where(i > 0, prev_ref[tm - halo:tm, :], 0.0)
    ext_ref[halo:halo + tm, :] = cur_ref[...]
    base = halo - (CONV_WIDTH - 1)
    span = xs_ref.shape[1]
    for c0 in range(0, ch, cc):
        for r in range(1, sub):
            xs_ref[r - 1] = ext_ref[r:r + span, c0:c0 + cc]
        for r0 in range(0, tm, rc):
            acc = jnp.zeros((rc, cc), F32) + cb_ref[:, c0:c0 + cc]
            for w in range(CONV_WIDTH):
                a, r = divmod(base + r0 + w, sub)
                if r == 0:
                    tap = ext_ref[a * sub:a * sub + rc, c0:c0 + cc]
                else:
                    tap = xs_ref[r - 1, a * sub:a * sub + rc, :]
                acc = acc + tap * cw_ref[w:w + 1, c0:c0 + cc]
            y_ref[r0:r0 + rc, c0:c0 + cc] = acc
    y = y_ref[...]
    mu = jnp.mean(y, axis=-1, keepdims=True)
    d = y - mu
    var = jnp.mean(d * d, axis=-1, keepdims=True)
    z = d * lax.rsqrt(var + EPS) * lg_ref[...] + lb_ref[...]
    o_ref[...] = (z * jax.nn.sigmoid(z)).astype(o_ref.dtype)


def conv_module(u, conv_w, conv_b, ln_g, ln_b):
    t, ch = u.shape
    tm = _tile(t, 256)
    halo = 32
    row = lambda v: v.reshape(1, ch).astype(F32)
    full = lambda shape: pl.BlockSpec(shape, lambda i: (0, 0))
    cc = min(512, ch)
    return pl.pallas_call(
        functools.partial(_conv_kernel, halo=halo, rc=64, cc=cc),
        grid=(t // tm,),
        in_specs=[pl.BlockSpec((tm, ch), lambda i: (i, 0)),
                  pl.BlockSpec((tm, ch), lambda i: (jnp.maximum(i - 1, 0), 0)),
                  full((CONV_WIDTH, ch)), full((1, ch)), full((1, ch)), full((1, ch))],
        out_specs=pl.BlockSpec((tm, ch), lambda i: (i, 0)),
        out_shape=jax.ShapeDtypeStruct((t, ch), MXU_DTYPE),
        scratch_shapes=[pltpu.VMEM((tm + halo, ch), F32), pltpu.VMEM((7, tm + halo - 8, cc), F32),
                        pltpu.VMEM((tm, ch), F32)],
        compiler_params=_params(("parallel",),
                                [((tm, ch), F32), ((tm, ch), F32), ((tm, ch), MXU_DTYPE)],
                                extra=6 * _nbytes((tm, ch), F32)),
        name="conv_module",
    )(u, u, conv_w.astype(F32), row(conv_b), row(ln_g), row(ln_b))


def _dot_t(a, b):
    return lax.dot_general(a, b, (((0,), (0,)), ((), ())), preferred_element_type=F32)


def _sort_desc(c):
    n = len(c)
    if n == 1:
        return list(c)
    return _merge_desc(_sort_desc(c[:n // 2]) + _sort_desc(c[n // 2:])[::-1])


def _merge_desc(c):
    c = list(c)
    n = len(c)
    d = n // 2
    while d >= 1:
        for a in range(n):
            if a & d == 0:
                c[a], c[a + d] = jnp.maximum(c[a], c[a + d]), jnp.minimum(c[a], c[a + d])
        d //= 2
    return c


def _kth_largest(score, k):
    ns = score.shape[0]
    sub = 8
    a = _sort_desc([score[j * sub:(j + 1) * sub, :] for j in range(ns // sub)])[:k]
    shift = sub // 2
    while True:
        b = [pltpu.roll(v, shift, 0) for v in a]
        if 2 * len(a) <= k:
            top = a + b[::-1]
        else:
            top = [jnp.maximum(a[j], b[len(a) - 1 - j]) for j in range(len(a))]
        if shift == 1:
            break
        a = _merge_desc(top)
        shift //= 2
    kth = top[0]
    for v in top[1:]:
        kth = jnp.minimum(kth, v)
    return kth[0:1, :]


def _nsa_kernel(q_ref, kc_ref, vc_ref, ks_ref, kw_ref, vs_ref, vw_ref, gt_ref, mt_ref, lt_ref, e_ref, o_ref,
                qx_ref, biasx_ref, m_ref, l_ref, acc_ref, out_ref, sa_ref, sb_ref, *, tq, n_top, wl):
    i = pl.program_id(1)
    t0 = i * tq
    nc = kc_ref.shape[1]
    ns = mt_ref.shape[0]
    nb = tq // SEL_BLOCK
    gsl = [slice(g * tq, (g + 1) * tq) for g in range(GROUP)]

    def gate(g, b):
        return gt_ref[0, g * 3 + b:g * 3 + b + 1, :]

    def staggered(scores, consume, ahead):
        pending = [scores(g) for g in range(min(ahead, GROUP))]
        for g in range(GROUP):
            if g + ahead < GROUP:
                pending.append(scores(g + ahead))
            consume(g, pending[g])

    for g in range(GROUP):
        qg = q_ref[:, g * HEAD_DIM:(g + 1) * HEAD_DIM].astype(F32)
        qx_ref[0:HEAD_DIM, gsl[g]] = qg.T.astype(MXU_DTYPE)
    qx_ref[HEAD_DIM:, :] = jnp.zeros((HEAD_DIM, GROUP * tq), MXU_DTYPE)
    t_q = t0 + lax.broadcasted_iota(jnp.int32, (1, tq), 1)

    cmp_end = lax.broadcasted_iota(jnp.int32, (nc, 1), 0) * CMP_STRIDE + (CMP_BLOCK - 1)
    cbias = jnp.where(cmp_end <= t_q, 0.0, NEG_INF)
    kc = kc_ref[0]
    vc = vc_ref[0]
    psums = []

    def cmp_scores(g):
        return jnp.dot(kc, qx_ref[0:HEAD_DIM, gsl[g]], preferred_element_type=F32) + cbias

    def cmp_consume(g, sm):
        mx = jnp.maximum(jnp.max(sm, axis=0, keepdims=True), 0.5 * NEG_INF)
        e = jnp.exp2(sm - mx)
        den = jnp.sum(e, axis=0, keepdims=True)
        p = e * (1.0 / jnp.where(den > 0.0, den, 1.0))
        out_ref[:, gsl[g]] = gate(g, 0) * _dot_t(vc, p.astype(MXU_DTYPE))
        psums.append(p)

    staggered(cmp_scores, cmp_consume, 1)
    psum = functools.reduce(lambda a, b: a + b, psums)

    p_hi = psum.astype(MXU_DTYPE)
    p_lo = (psum - p_hi.astype(F32)).astype(MXU_DTYPE)
    imp = (jnp.dot(mt_ref[...], p_hi, preferred_element_type=F32)
           + jnp.dot(mt_ref[...], p_lo, preferred_element_type=F32))
    s_id = lax.broadcasted_iota(jnp.int32, (ns, 1), 0)
    cur = t_q // SEL_BLOCK
    forced = (s_id == 0) | (s_id == cur) | (s_id == cur - 1)
    valid = s_id * SEL_BLOCK <= t_q
    score = jnp.where(forced, FORCE, jnp.where(valid, imp, NEG_INF))
    kth = _kth_largest(score, n_top)
    above = score > kth
    equal = score == kth
    n_above = jnp.sum(jnp.where(above, 1.0, 0.0), axis=0, keepdims=True)
    n_equal_upto = jnp.dot(lt_ref[...], jnp.where(equal, 1.0, 0.0).astype(MXU_DTYPE),
                           preferred_element_type=F32)
    picked = above | (equal & (n_equal_upto <= n_top - n_above))
    block_bias = jnp.where(picked, 0.0, NEG_INF)
    pad_rows = jnp.zeros((biasx_ref.shape[1] - nb, tq), F32)
    for j in range(ns // nb):
        slab = jnp.concatenate([block_bias[j * nb:(j + 1) * nb, :], pad_rows], axis=0)
        biasx_ref[j] = jnp.concatenate([slab] * GROUP, axis=1).astype(MXU_DTYPE)

    ws = pl.multiple_of(jnp.maximum(t0 + tq - wl, 0), SEL_BLOCK)
    back = t_q - (ws + lax.broadcasted_iota(jnp.int32, (wl, 1), 0))
    wbias = jnp.where(lax.bitcast_convert_type(back, jnp.uint32) < WINDOW, 0.0, NEG_INF)
    kw = kw_ref[pl.ds(ws, wl), :]
    vw = vw_ref[pl.ds(ws, wl), :]

    def win_scores(g):
        return jnp.dot(kw, qx_ref[0:HEAD_DIM, gsl[g]], preferred_element_type=F32) + wbias

    def win_consume(g, sm):
        e = jnp.exp2(sm - jnp.max(sm, axis=0, keepdims=True))
        inv = 1.0 / jnp.sum(e, axis=0, keepdims=True)
        out_ref[:, gsl[g]] += (gate(g, 2) * inv) * _dot_t(vw, e.astype(MXU_DTYPE))

    staggered(win_scores, win_consume, 1)

    m_ref[...] = jnp.full(m_ref.shape, NEG_INF, F32)
    l_ref[...] = jnp.zeros(l_ref.shape, F32)
    acc_ref[...] = jnp.zeros(acc_ref.shape, F32)

    def issue_scores(kt, s_ref):
        k0 = pl.multiple_of(kt * tq, tq)
        kx = jnp.concatenate([ks_ref[pl.ds(k0, tq), :], e_ref[...]], axis=1)
        qx_ref[HEAD_DIM:HEAD_DIM + biasx_ref.shape[1], :] = biasx_ref[kt]
        for g in range(GROUP):
            s_ref[g] = jnp.dot(kx, qx_ref[:, gsl[g]], preferred_element_type=F32)

    def consume(kt, s_ref, diagonal):
        k0 = pl.multiple_of(kt * tq, tq)
        vv = vs_ref[pl.ds(k0, tq), :]
        if diagonal:
            causal = k0 + lax.broadcasted_iota(jnp.int32, (tq, 1), 0) <= t_q
        for g in range(GROUP):
            sm = jnp.where(causal, s_ref[g], NEG_INF) if diagonal else s_ref[g]
            m_old = m_ref[:, gsl[g]]
            m_new = jnp.maximum(m_old, jnp.max(sm, axis=0, keepdims=True))
            alpha = jnp.exp2(m_old - m_new)
            pt = jnp.exp2(sm - m_new)
            l_ref[:, gsl[g]] = alpha * l_ref[:, gsl[g]] + jnp.sum(pt, axis=0, keepdims=True)
            acc_ref[:, gsl[g]] = alpha * acc_ref[:, gsl[g]] + _dot_t(vv, pt.astype(MXU_DTYPE))
            m_ref[:, gsl[g]] = m_new

    def stage(kt, s_cur, s_next):
        issue_scores(kt + 1, s_next)
        consume(kt, s_cur, False)

    issue_scores(0, sa_ref)

    def pair(j, carry):
        stage(2 * j, sa_ref, sb_ref)
        stage(2 * j + 1, sb_ref, sa_ref)
        return carry

    lax.fori_loop(0, i // 2, pair, 0)

    @pl.when(i % 2 == 0)
    def _():
        consume(i, sa_ref, True)

    @pl.when(i % 2 == 1)
    def _():
        stage(i - 1, sa_ref, sb_ref)
        consume(i, sb_ref, True)

    for g in range(GROUP):
        out_t = out_ref[:, gsl[g]] + (gate(g, 1) / l_ref[:, gsl[g]]) * acc_ref[:, gsl[g]]
        o_ref[:, g * HEAD_DIM:(g + 1) * HEAD_DIM] = out_t.T.astype(o_ref.dtype)


def nsa_attention(q, k_cmp, v_cmp, k_sw, v_sw, gates_t, tq=256):
    t = q.shape[0]
    tq = min(tq, t)
    wl = min(WINDOW + tq, t)
    nc = k_cmp.shape[1]
    ns = t // SEL_BLOCK
    n_top = min(N_SELECT, ns)
    r = GROUP * tq
    lt = jnp.asarray(np.tril(np.ones((ns, ns))), MXU_DTYPE)
    cj = np.arange(nc)[None, :] * CMP_STRIDE
    si = np.arange(ns)[:, None] * SEL_BLOCK
    ov = np.clip(np.minimum(cj + CMP_BLOCK, si + SEL_BLOCK) - np.maximum(cj, si), 0, None) / CMP_BLOCK
    ov[:, (t - CMP_BLOCK) // CMP_STRIDE + 1:] = 0.0
    mt = jnp.asarray(ov, MXU_DTYPE)
    nb = tq // SEL_BLOCK
    onehot = jnp.asarray(np.arange(tq)[:, None] // SEL_BLOCK == np.arange(HEAD_DIM)[None, :], MXU_DTYPE)
    bias_rows = 16
    assert nb <= bias_rows and ns % nb == 0

    head = lambda off: pl.BlockSpec((t, HEAD_DIM), lambda h, i: (0, off + h))
    return pl.pallas_call(
        functools.partial(_nsa_kernel, tq=tq, n_top=n_top, wl=wl),
        grid=(N_KV_HEADS, t // tq),
        in_specs=[pl.BlockSpec((tq, GROUP * HEAD_DIM), lambda h, i: (i, h)),
                  pl.BlockSpec((1, nc, HEAD_DIM), lambda h, i: (h, 0, 0)),
                  pl.BlockSpec((1, nc, HEAD_DIM), lambda h, i: (h, 0, 0)),
                  head(0), head(N_KV_HEADS), head(0), head(N_KV_HEADS),
                  pl.BlockSpec((1, 16, tq), lambda h, i: (h, 0, i)),
                  pl.BlockSpec((ns, nc), lambda h, i: (0, 0)),
                  pl.BlockSpec((ns, ns), lambda h, i: (0, 0)),
                  pl.BlockSpec((tq, HEAD_DIM), lambda h, i: (0, 0))],
        out_specs=pl.BlockSpec((tq, GROUP * HEAD_DIM), lambda h, i: (i, h)),
        out_shape=jax.ShapeDtypeStruct((t, Q_W), MXU_DTYPE),
        scratch_shapes=[pltpu.VMEM((2 * HEAD_DIM, r), MXU_DTYPE),
                        pltpu.VMEM((ns // nb, bias_rows, r), MXU_DTYPE),
                        pltpu.VMEM((1, r), F32), pltpu.VMEM((1, r), F32),
                        pltpu.VMEM((HEAD_DIM, r), F32), pltpu.VMEM((HEAD_DIM, r), F32),
                        pltpu.VMEM((GROUP, tq, tq), F32), pltpu.VMEM((GROUP, tq, tq), F32)],
        compiler_params=_params(("parallel", "arbitrary"),
                                [((t, HEAD_DIM), MXU_DTYPE)] * 4 + [((nc, HEAD_DIM), MXU_DTYPE)] * 2
                                + [((tq, GROUP * HEAD_DIM), MXU_DTYPE)] * 2 + [((ns, nc), MXU_DTYPE)],
                                extra=12 * _nbytes((max(nc, wl), tq), F32)),
        name="nsa_attention",
    )(q, k_cmp, v_cmp, k_sw, k_sw, v_sw, v_sw, gates_t, mt, lt, onehot)


def _pad_cols(w, n):
    return jnp.pad(w, ((0, 0), (0, n - w.shape[1])))


def _ffn(x, norm_g, w_gate, w_up, w_down):
    f = w_gate.shape[1]
    fp = -(-f // 1024) * 1024 if f > 1024 else f
    wd = jnp.concatenate([w_down.astype(MXU_DTYPE), jnp.zeros((fp - f, w_down.shape[1]), MXU_DTYPE)], axis=0)
    xn = rmsnorm_cast(x, norm_g)
    h = dual_mm(xn, w_gate, w_up, jax.nn.silu, MXU_DTYPE, "ffn_gate_up", tn=256, n_out=fp)
    tk = fp // 4 if (fp // 4) % LANES == 0 else fp
    return mm_resid(h, wd, x, 0.5, tk, "ffn_down")


def _layer(x, p):
    t, d = x.shape
    x = _ffn(x, p["ffn1_norm"], p["ffn1_w_gate"], p["ffn1_w_up"], p["ffn1_w_down"])

    hn = rmsnorm_cast(x, p["mix_norm"])
    w_in = p["w_in"]
    conv_ch = p["conv_w"].shape[1]
    assert Q_W % KV_W == 0
    qb = Q_W // KV_W
    o = Q_W + 6 * KV_W
    part = lambda width: w_in[:, o:o + width].astype(MXU_DTYPE)
    wgn = part(NSA_GATE_W); o += NSA_GATE_W
    wga = part(conv_ch); o += conv_ch
    wgg = part(conv_ch); o += conv_ch
    wm0 = part(d); o += d
    wm1 = part(d); o += d

    scale = HEAD_DIM ** -0.5 * np.log2(np.e)
    q = mm(hn, w_in, "headnorm", MXU_DTYPE, "proj_q", gain=jnp.tile(p["q_norm"] * scale, N_HEADS), n=Q_W)
    k_sw = mm(hn, w_in, "headnorm", MXU_DTYPE, "proj_k", n=2 * KV_W, wcol=lambda j: qb + 2 + 2 * j,
              gain=jnp.concatenate([jnp.tile(p["k_norm"][1], N_KV_HEADS), jnp.tile(p["k_norm"][2], N_KV_HEADS)]))
    v_sw = mm(hn, w_in, "plain", MXU_DTYPE, "proj_v", n=2 * KV_W, wcol=lambda j: qb + 3 + 2 * j)
    kv_c = mm(hn, w_in, "plain", F32, "proj_kv_cmp", n=2 * KV_W, wcol=lambda j: qb + j)
    g_nsa = mm(hn, _pad_cols(wgn, LANES), "sigmoid", F32, "proj_gates")
    u = dual_mm(hn, wgg, wga, jax.nn.sigmoid, F32, "proj_glu")

    chunks = kv_c.reshape(t // CMP_STRIDE, CMP_STRIDE, 2, N_KV_HEADS, HEAD_DIM)
    chunks = chunks.transpose(2, 3, 0, 1, 4).reshape(2, N_KV_HEADS, t // CMP_STRIDE, CMP_STRIDE * HEAD_DIM)
    k_cmp = compress(chunks[0], p["cmp_pos_k"], p["cmp_k_w1"], p["cmp_k_w2"], p["k_norm"][0], True, "compress_k")
    v_cmp = compress(chunks[1], p["cmp_pos_v"], p["cmp_v_w1"], p["cmp_v_w2"], p["k_norm"][0], False, "compress_v")

    gates_t = g_nsa[:, :NSA_GATE_W].reshape(t, N_KV_HEADS, GROUP * 3).transpose(1, 2, 0)
    gates_t = jnp.pad(gates_t, ((0, 0), (0, 16 - GROUP * 3), (0, 0)))
    attn = nsa_attention(q, k_cmp, v_cmp, k_sw, v_sw, gates_t)

    uc = conv_module(u, p["conv_w"], p["conv_b"], p["conv_ln_g"], p["conv_ln_b"])

    mix = gated_mm(attn, p["nsa_w_o"], hn, wm0, None, F32, "merge_nsa")
    mix = gated_mm(uc, p["conv_w_o"], hn, wm1, mix, MXU_DTYPE, "merge_conv")
    x = mm_resid(mix, p["w_out"], x, 1.0, d, "mix_out", tn=512)

    return _ffn(x, p["ffn2_norm"], p["ffn2_w_gate"], p["ffn2_w_up"], p["ffn2_w_down"])


_NAMES = ("ffn1_norm", "ffn1_w_gate", "ffn1_w_up", "ffn1_w_down", "mix_norm", "w_in", "q_norm", "k_norm",
          "cmp_pos_k", "cmp_k_w1", "cmp_k_w2", "cmp_pos_v", "cmp_v_w1", "cmp_v_w2", "nsa_w_o",
          "conv_w", "conv_b", "conv_ln_g", "conv_ln_b", "conv_w_o", "w_out",
          "ffn2_norm", "ffn2_w_gate", "ffn2_w_up", "ffn2_w_down")


def kernel(x, ffn1_norm, ffn1_w_gate, ffn1_w_up, ffn1_w_down, mix_norm, w_in, q_norm, k_norm, cmp_pos_k, cmp_k_w1, cmp_k_w2, cmp_pos_v, cmp_v_w1, cmp_v_w2, nsa_w_o, conv_w, conv_b, conv_ln_g, conv_ln_b, conv_w_o, w_out, ffn2_norm, ffn2_w_gate, ffn2_w_up, ffn2_w_down):
    params = (ffn1_norm, ffn1_w_gate, ffn1_w_up, ffn1_w_down, mix_norm, w_in, q_norm, k_norm, cmp_pos_k,
              cmp_k_w1, cmp_k_w2, cmp_pos_v, cmp_v_w1, cmp_v_w2, nsa_w_o, conv_w, conv_b, conv_ln_g,
              conv_ln_b, conv_w_o, w_out, ffn2_norm, ffn2_w_gate, ffn2_w_up, ffn2_w_down)
    b, t, d = x.shape
    depth = ffn1_norm.shape[0]
    xs = x.reshape(b * t, d)
    outs = []
    for bi in range(b):
        xb = xs[bi * t:(bi + 1) * t]
        for l in range(depth):
            xb = _layer(xb, {n: a[l] for n, a in zip(_NAMES, params)})
        outs.append(xb)
    out = outs[0] if b == 1 else jnp.concatenate(outs, axis=0)
    return out.reshape(b, t, d)
```

```python
import functools
from typing import NamedTuple, Optional

import numpy as np
import jax
import jax.numpy as jnp
from jax import lax
from jax.experimental import pallas as pl
from jax.experimental.pallas import tpu as pltpu

HEAD_DIM = 128
N_HEADS = 16
N_KV_HEADS = 4
GROUP = N_HEADS // N_KV_HEADS
CMP_BLOCK = 32
CMP_STRIDE = 16
SEL_BLOCK = 64
N_SELECT = 16
WINDOW = 512
CONV_WIDTH = 31
NEG_INF = -1e30
FORCE = 1e9
EPS = 1e-6

Q_W = N_HEADS * HEAD_DIM
KV_W = N_KV_HEADS * HEAD_DIM
NSA_GATE_W = N_HEADS * 3

MXU_DTYPE = jnp.bfloat16
LANES = 128
V7X_VMEM_BYTES = 64 * 1024 * 1024
VMEM_CAP = V7X_VMEM_BYTES - 6 * 1024 * 1024

F32 = jnp.float32


def _nbytes(shape, dtype):
    return int(np.prod(shape)) * jnp.dtype(dtype).itemsize


def _params(semantics, blocks, extra=0):
    need = 2 * sum(_nbytes(s, d) for s, d in blocks) + extra + (2 << 20)
    return pltpu.CompilerParams(dimension_semantics=semantics,
                                vmem_limit_bytes=int(min(max(need, 16 << 20), VMEM_CAP)))


def _tile(n, pref):
    return pref if n % pref == 0 else n


def _rmsnorm_kernel(x_ref, g_ref, o_ref):
    x = x_ref[...]
    ms = jnp.mean(x * x, axis=-1, keepdims=True)
    o_ref[...] = (x * lax.rsqrt(ms + EPS) * g_ref[...]).astype(o_ref.dtype)


def rmsnorm_cast(x, gain):
    m, d = x.shape
    tm = _tile(m, 256)
    return pl.pallas_call(
        _rmsnorm_kernel,
        grid=(m // tm,),
        in_specs=[pl.BlockSpec((tm, d), lambda i: (i, 0)),
                  pl.BlockSpec((1, d), lambda i: (0, 0))],
        out_specs=pl.BlockSpec((tm, d), lambda i: (i, 0)),
        out_shape=jax.ShapeDtypeStruct((m, d), MXU_DTYPE),
        compiler_params=_params(("parallel",), [((tm, d), F32), ((tm, d), MXU_DTYPE)],
                                extra=2 * _nbytes((tm, d), F32)),
        name="rmsnorm_cast",
    )(x, gain.reshape(1, d).astype(F32))


class Weight(NamedTuple):
    arr: jax.Array
    row0: Optional[int] = None


def _w_spec(w, k, tn, block_of):
    if w.row0 is None:
        return pl.BlockSpec((k, tn), lambda i, j: (0, block_of(j)))
    assert w.row0 % 8 == 0 and tn % 8 == 0
    return pl.BlockSpec((pl.Element(tn), pl.Element(k)),
                        lambda i, j: (pl.multiple_of(w.row0 + block_of(j) * tn, 8), 0))


def _w_block(w, k, tn):
    return ((k, tn) if w.row0 is None else (tn, k)), w.arr.dtype


def _mxu(x, w_ref, transposed):
    w = w_ref[...].astype(x.dtype)
    if transposed:
        return lax.dot_general(x, w, (((1,), (1,)), ((), ())), preferred_element_type=F32)
    return jnp.dot(x, w, preferred_element_type=F32)


def _dual_kernel(x_ref, w1_ref, w2_ref, o_ref, *, act, transposed):
    x = x_ref[...]
    a = _mxu(x, w1_ref, transposed)
    b = _mxu(x, w2_ref, transposed)
    o_ref[...] = (act(a) * b).astype(o_ref.dtype)


def dual_mm(x, w1, w2, n, act, out_dtype, name, tn=512):
    m, k = x.shape
    tm, tn = _tile(m, 1024), _tile(n, tn)
    assert n % tn == 0 and (w1.row0 is None) == (w2.row0 is None)
    block_of = lambda j: j
    return pl.pallas_call(
        functools.partial(_dual_kernel, act=act, transposed=w1.row0 is not None),
        grid=(m // tm, n // tn),
        in_specs=[pl.BlockSpec((tm, k), lambda i, j: (i, 0)),
                  _w_spec(w1, k, tn, block_of),
                  _w_spec(w2, k, tn, block_of)],
        out_specs=pl.BlockSpec((tm, tn), lambda i, j: (i, j)),
        out_shape=jax.ShapeDtypeStruct((m, n), out_dtype),
        compiler_params=_params(("parallel", "arbitrary"),
                                [((tm, k), x.dtype), _w_block(w1, k, tn), _w_block(w2, k, tn),
                                 ((tm, tn), out_dtype)],
                                extra=4 * _nbytes((tm, tn), F32)),
        name=name,
    )(x, w1.arr, w2.arr)


def _mm_kernel(x_ref, w_ref, g_ref, o_ref, *, mode, transposed):
    acc = _mxu(x_ref[...], w_ref, transposed)
    if mode == "headnorm":
        for c in range(acc.shape[1] // HEAD_DIM):
            sl = slice(c * HEAD_DIM, (c + 1) * HEAD_DIM)
            v = acc[:, sl]
            ms = jnp.mean(v * v, axis=-1, keepdims=True)
            o_ref[:, sl] = (v * lax.rsqrt(ms + EPS) * g_ref[:, sl]).astype(o_ref.dtype)
    elif mode == "sigmoid":
        o_ref[...] = jax.nn.sigmoid(acc).astype(o_ref.dtype)
    else:
        o_ref[...] = acc.astype(o_ref.dtype)


def mm(x, w, n, mode, out_dtype, name, gain=None, block_of=None):
    m, k = x.shape
    tm, tn = _tile(m, 1024), _tile(n, 512)
    if gain is None:
        gain = jnp.ones((1, n), F32)
    if block_of is None:
        block_of = lambda j: j
    return pl.pallas_call(
        functools.partial(_mm_kernel, mode=mode, transposed=w.row0 is not None),
        grid=(m // tm, n // tn),
        in_specs=[pl.BlockSpec((tm, k), lambda i, j: (i, 0)),
                  _w_spec(w, k, tn, block_of),
                  pl.BlockSpec((1, tn), lambda i, j: (0, j))],
        out_specs=pl.BlockSpec((tm, tn), lambda i, j: (i, j)),
        out_shape=jax.ShapeDtypeStruct((m, n), out_dtype),
        compiler_params=_params(("parallel", "arbitrary"),
                                [((tm, k), x.dtype), _w_block(w, k, tn), ((tm, tn), out_dtype)],
                                extra=3 * _nbytes((tm, tn), F32)),
        name=name,
    )(x, w.arr, gain.reshape(1, n).astype(F32))


def _mm_acc_kernel(*refs, scale, nk, has_tail):
    if has_tail:
        h_ref, w_ref, ht_ref, wt_ref, r_ref, o_ref, acc_ref = refs
    else:
        h_ref, w_ref, r_ref, o_ref, acc_ref = refs
    kk = pl.program_id(2)

    @pl.when(kk == 0)
    def _():
        if has_tail:
            acc_ref[...] = _mxu(ht_ref[...], wt_ref, False)
        else:
            acc_ref[...] = jnp.zeros_like(acc_ref)

    acc_ref[...] += _mxu(h_ref[...], w_ref, False)

    @pl.when(kk == nk - 1)
    def _():
        o_ref[...] = (r_ref[...] + scale * acc_ref[...]).astype(o_ref.dtype)


def _k_tiling(k, nk):
    if k % (nk * LANES) == 0:
        return k // nk, nk, 0
    tk = k // nk // LANES * LANES
    tail = k - nk * tk
    if tk > 0 and tail % LANES == 0 and (nk * tk) % tail == 0:
        return tk, nk, tail
    return k, 1, 0


def mm_resid(h, w, resid, scale, nk, name, tn=1024):
    m, k = h.shape
    n = w.shape[1]
    tm, tn = _tile(m, 1024), _tile(n, tn)
    tk, nk, tail = _k_tiling(k, nk)
    in_specs = [pl.BlockSpec((tm, tk), lambda i, j, kk: (i, kk)),
                pl.BlockSpec((tk, tn), lambda i, j, kk: (kk, j))]
    blocks = [((tm, tk), h.dtype), ((tk, tn), w.dtype), ((tm, tn), F32), ((tm, tn), F32)]
    args = [h, w]
    if tail:
        tb = nk * tk // tail
        in_specs += [pl.BlockSpec((tm, tail), lambda i, j, kk: (i, tb)),
                     pl.BlockSpec((tail, tn), lambda i, j, kk: (tb, j))]
        blocks += [((tm, tail), h.dtype), ((tail, tn), w.dtype)]
        args += [h, w]
    in_specs.append(pl.BlockSpec((tm, tn), lambda i, j, kk: (i, j)))
    return pl.pallas_call(
        functools.partial(_mm_acc_kernel, scale=scale, nk=nk, has_tail=bool(tail)),
        grid=(m // tm, n // tn, nk),
        in_specs=in_specs,
        out_specs=pl.BlockSpec((tm, tn), lambda i, j, kk: (i, j)),
        out_shape=jax.ShapeDtypeStruct((m, n), F32),
        scratch_shapes=[pltpu.VMEM((tm, tn), F32)],
        compiler_params=_params(("parallel", "arbitrary", "arbitrary"), blocks,
                                extra=3 * _nbytes((tm, tn), F32)),
        name=name,
    )(*args, resid)


def _gated_kernel(*refs, has_prev, wo_transposed, wm_transposed):
    if has_prev:
        a_ref, wo_ref, h_ref, wm_ref, p_ref, o_ref = refs
    else:
        a_ref, wo_ref, h_ref, wm_ref, o_ref = refs
    y = _mxu(a_ref[...], wo_ref, wo_transposed)
    g = jax.nn.sigmoid(_mxu(h_ref[...], wm_ref, wm_transposed))
    r = g * y
    if has_prev:
        r = p_ref[...] + r
    o_ref[...] = r.astype(o_ref.dtype)


def gated_mm(a, wo, h, wm, n, prev, out_dtype, name):
    m, ka = a.shape
    kh = h.shape[1]
    tm, tn = _tile(m, 512), _tile(n, 512)
    has_prev = prev is not None
    ident = lambda j: j
    in_specs = [pl.BlockSpec((tm, ka), lambda i, j: (i, 0)),
                _w_spec(wo, ka, tn, ident),
                pl.BlockSpec((tm, kh), lambda i, j: (i, 0)),
                _w_spec(wm, kh, tn, ident)]
    blocks = [((tm, ka), a.dtype), _w_block(wo, ka, tn), ((tm, kh), h.dtype), _w_block(wm, kh, tn),
              ((tm, tn), out_dtype)]
    args = [a, wo.arr, h, wm.arr]
    if has_prev:
        in_specs.append(pl.BlockSpec((tm, tn), lambda i, j: (i, j)))
        blocks.append(((tm, tn), prev.dtype))
        args.append(prev)
    return pl.pallas_call(
        functools.partial(_gated_kernel, has_prev=has_prev, wo_transposed=wo.row0 is not None,
                          wm_transposed=wm.row0 is not None),
        grid=(m // tm, n // tn),
        in_specs=in_specs,
        out_specs=pl.BlockSpec((tm, tn), lambda i, j: (i, j)),
        out_shape=jax.ShapeDtypeStruct((m, n), out_dtype),
        compiler_params=_params(("parallel", "arbitrary"), blocks, extra=4 * _nbytes((tm, tn), F32)),
        name=name,
    )(*args)


def _gelu_tanh(x):
    return 0.5 * x * (1.0 + jnp.tanh(np.sqrt(2.0 / np.pi).astype(np.float32) * (x + 0.044715 * (x * x * x))))


def _compress_kernel(a_ref, pa_ref, pb_ref, w1a_ref, w1b_ref, w2_ref, g_ref, o_ref, *, norm):
    a = a_ref[0]
    nch = a.shape[0]
    ua = jnp.dot((a + pa_ref[...]).astype(MXU_DTYPE), w1a_ref[...], preferred_element_type=F32)
    ub = jnp.dot((a + pb_ref[...]).astype(MXU_DTYPE), w1b_ref[...], preferred_element_type=F32)
    pre = ua + pltpu.roll(ub, nch - 1, 0)
    hid = _gelu_tanh(pre).astype(MXU_DTYPE)
    out = jnp.dot(hid, w2_ref[...], preferred_element_type=F32)
    if norm:
        ms = jnp.mean(out * out, axis=-1, keepdims=True)
        out = out * lax.rsqrt(ms + EPS) * g_ref[...]
    o_ref[0] = out.astype(o_ref.dtype)


def compress(chunks, pos, w1, w2, gain, norm, name):
    hkv, nch, cw = chunks.shape
    hid = w1.shape[1]
    posf = pos.reshape(1, CMP_BLOCK * HEAD_DIM).astype(F32)
    pa, pb = posf[:, :cw], posf[:, cw:]
    w1c = w1.astype(MXU_DTYPE)
    w1a, w1b = w1c[:cw], w1c[cw:]
    full = lambda shape: pl.BlockSpec(shape, lambda h: (0,) * len(shape))
    return pl.pallas_call(
        functools.partial(_compress_kernel, norm=norm),
        grid=(hkv,),
        in_specs=[pl.BlockSpec((1, nch, cw), lambda h: (h, 0, 0)),
                  full((1, cw)), full((1, cw)), full((cw, hid)), full((cw, hid)),
                  full((hid, HEAD_DIM)), full((1, HEAD_DIM))],
        out_specs=pl.BlockSpec((1, nch, HEAD_DIM), lambda h: (h, 0, 0)),
        out_shape=jax.ShapeDtypeStruct((hkv, nch, HEAD_DIM), MXU_DTYPE),
        compiler_params=_params(("parallel",),
                                [((nch, cw), F32), ((cw, hid), MXU_DTYPE), ((cw, hid), MXU_DTYPE)],
                                extra=8 * _nbytes((nch, cw), F32)),
        name=name,
    )(chunks, pa, pb, w1a, w1b, w2.astype(MXU_DTYPE), gain.reshape(1, HEAD_DIM).astype(F32))


def _conv_kernel(cur_ref, prev_ref, cw_ref, cb_ref, lg_ref, lb_ref, o_ref, ext_ref, xs_ref, y_ref, *,
                 halo, rc, cc):
    i = pl.program_id(0)
    tm, ch = cur_ref.shape
    sub = 8
    ext_ref[0:halo, :] = jnp.where(i > 0, prev_ref[tm - halo:tm, :], 0.0)
    ext_ref[halo:halo + tm, :] = cur_ref[...]
    base = halo - (CONV_WIDTH - 1)
    span = xs_ref.shape[1]
    for c0 in range(0, ch, cc):
        for r in range(1, sub):
            xs_ref[r - 1] = ext_ref[r:r + span, c0:c0 + cc]
        for r0 in range(0, tm, rc):
            acc = jnp.zeros((rc, cc), F32) + cb_ref[:, c0:c0 + cc]
            for w in range(CONV_WIDTH):
                a, r = divmod(base + r0 + w, sub)
                if r == 0:
                    tap = ext_ref[a * sub:a * sub + rc, c0:c0 + cc]
                else:
                    tap = xs_ref[r - 1, a * sub:a * sub + rc, :]
                acc = acc + tap * cw_ref[w:w + 1, c0:c0 + cc]
            y_ref[r0:r0 + rc, c0:c0 + cc] = acc
    y = y_ref[...]
    mu = jnp.mean(y, axis=-1, keepdims=True)
    d = y - mu
    var = jnp.mean(d * d, axis=-1, keepdims=True)
    z = d * lax.rsqrt(var + EPS) * lg_ref[...] + lb_ref[...]
    o_ref[...] = (z * jax.nn.sigmoid(z)).astype(o_ref.dtype)


def conv_module(u, conv_w, conv_b, ln_g, ln_b):
    t, ch = u.shape
    tm = _tile(t, 256)
    halo = 32
    row = lambda v: v.reshape(1, ch).astype(F32)
    full = lambda shape: pl.BlockSpec(shape, lambda i: (0, 0))
    cc = min(512, ch)
    return pl.pallas_call(
        functools.partial(_conv_kernel, halo=halo, rc=64, cc=cc),
        grid=(t // tm,),
        in_specs=[pl.BlockSpec((tm, ch), lambda i: (i, 0)),
                  pl.BlockSpec((tm, ch), lambda i: (jnp.maximum(i - 1, 0), 0)),
                  full((CONV_WIDTH, ch)), full((1, ch)), full((1, ch)), full((1, ch))],
        out_specs=pl.BlockSpec((tm, ch), lambda i: (i, 0)),
        out_shape=jax.ShapeDtypeStruct((t, ch), MXU_DTYPE),
        scratch_shapes=[pltpu.VMEM((tm + halo, ch), F32), pltpu.VMEM((7, tm + halo - 8, cc), F32),
                        pltpu.VMEM((tm, ch), F32)],
        compiler_params=_params(("parallel",),
                                [((tm, ch), F32), ((tm, ch), F32), ((tm, ch), MXU_DTYPE)],
                                extra=6 * _nbytes((tm, ch), F32)),
        name="conv_module",
    )(u, u, conv_w.astype(F32), row(conv_b), row(ln_g), row(ln_b))


def _dot_t(a, b):
    return lax.dot_general(a, b, (((0,), (0,)), ((), ())), preferred_element_type=F32)


def _sort_desc(c):
    n = len(c)
    if n == 1:
        return list(c)
    return _merge_desc(_sort_desc(c[:n // 2]) + _sort_desc(c[n // 2:])[::-1])


def _merge_desc(c):
    c = list(c)
    n = len(c)
    d = n // 2
    while d >= 1:
        for a in range(n):
            if a & d == 0:
                c[a], c[a + d] = jnp.maximum(c[a], c[a + d]), jnp.minimum(c[a], c[a + d])
        d //= 2
    return c


def _kth_largest(score, k):
    ns = score.shape[0]
    sub = 8
    a = _sort_desc([score[j * sub:(j + 1) * sub, :] for j in range(ns // sub)])[:k]
    shift = sub // 2
    while True:
        b = [pltpu.roll(v, shift, 0) for v in a]
        if 2 * len(a) <= k:
            top = a + b[::-1]
        else:
            top = [jnp.maximum(a[j], b[len(a) - 1 - j]) for j in range(len(a))]
        if shift == 1:
            break
        a = _merge_desc(top)
        shift //= 2
    kth = top[0]
    for v in top[1:]:
        kth = jnp.minimum(kth, v)
    return kth[0:1, :]


def _nsa_kernel(q_ref, kc_ref, vc_ref, ks_ref, kw_ref, vs_ref, vw_ref, gt_ref, mt_ref, lt_ref, e_ref, o_ref,
                qx_ref, biasx_ref, m_ref, l_ref, acc_ref, out_ref, sa_ref, sb_ref, *, tq, n_top, wl):
    i = pl.program_id(1)
    t0 = i * tq
    nc = kc_ref.shape[1]
    ns = mt_ref.shape[0]
    nb = tq // SEL_BLOCK
    gsl = [slice(g * tq, (g + 1) * tq) for g in range(GROUP)]

    def gate(g, b):
        return gt_ref[0, g * 3 + b:g * 3 + b + 1, :]

    def staggered(scores, consume, ahead):
        pending = [scores(g) for g in range(min(ahead, GROUP))]
        for g in range(GROUP):
            if g + ahead < GROUP:
                pending.append(scores(g + ahead))
            consume(g, pending[g])

    for g in range(GROUP):
        qg = q_ref[:, g * HEAD_DIM:(g + 1) * HEAD_DIM].astype(F32)
        qx_ref[0:HEAD_DIM, gsl[g]] = qg.T.astype(MXU_DTYPE)
    qx_ref[HEAD_DIM:, :] = jnp.zeros((HEAD_DIM, GROUP * tq), MXU_DTYPE)
    t_q = t0 + lax.broadcasted_iota(jnp.int32, (1, tq), 1)

    cmp_end = lax.broadcasted_iota(jnp.int32, (nc, 1), 0) * CMP_STRIDE + (CMP_BLOCK - 1)
    cbias = jnp.where(cmp_end <= t_q, 0.0, NEG_INF)
    kc = kc_ref[0]
    vc = vc_ref[0]
    psums = []

    def cmp_scores(g):
        return jnp.dot(kc, qx_ref[0:HEAD_DIM, gsl[g]], preferred_element_type=F32) + cbias

    def cmp_consume(g, sm):
        mx = jnp.maximum(jnp.max(sm, axis=0, keepdims=True), 0.5 * NEG_INF)
        e = jnp.exp2(sm - mx)
        den = jnp.sum(e, axis=0, keepdims=True)
        p = e * (1.0 / jnp.where(den > 0.0, den, 1.0))
        out_ref[:, gsl[g]] = gate(g, 0) * _dot_t(vc, p.astype(MXU_DTYPE))
        psums.append(p)

    staggered(cmp_scores, cmp_consume, 1)
    psum = functools.reduce(lambda a, b: a + b, psums)

    p_hi = psum.astype(MXU_DTYPE)
    p_lo = (psum - p_hi.astype(F32)).astype(MXU_DTYPE)
    imp = (jnp.dot(mt_ref[...], p_hi, preferred_element_type=F32)
           + jnp.dot(mt_ref[...], p_lo, preferred_element_type=F32))
    s_id = lax.broadcasted_iota(jnp.int32, (ns, 1), 0)
    cur = t_q // SEL_BLOCK
    forced = (s_id == 0) | (s_id == cur) | (s_id == cur - 1)
    valid = s_id * SEL_BLOCK <= t_q
    score = jnp.where(forced, FORCE, jnp.where(valid, imp, NEG_INF))
    kth = _kth_largest(score, n_top)
    above = score > kth
    equal = score == kth
    n_above = jnp.sum(jnp.where(above, 1.0, 0.0), axis=0, keepdims=True)
    n_equal_upto = jnp.dot(lt_ref[...], jnp.where(equal, 1.0, 0.0).astype(MXU_DTYPE),
                           preferred_element_type=F32)
    picked = above | (equal & (n_equal_upto <= n_top - n_above))
    block_bias = jnp.where(picked, 0.0, NEG_INF)
    pad_rows = jnp.zeros((biasx_ref.shape[1] - nb, tq), F32)
    for j in range(ns // nb):
        slab = jnp.concatenate([block_bias[j * nb:(j + 1) * nb, :], pad_rows], axis=0)
        biasx_ref[j] = jnp.concatenate([slab] * GROUP, axis=1).astype(MXU_DTYPE)

    ws = pl.multiple_of(jnp.maximum(t0 + tq - wl, 0), SEL_BLOCK)
    back = t_q - (ws + lax.broadcasted_iota(jnp.int32, (wl, 1), 0))
    wbias = jnp.where(lax.bitcast_convert_type(back, jnp.uint32) < WINDOW, 0.0, NEG_INF)
    kw = kw_ref[pl.ds(ws, wl), :]
    vw = vw_ref[pl.ds(ws, wl), :]

    def win_scores(g):
        return jnp.dot(kw, qx_ref[0:HEAD_DIM, gsl[g]], preferred_element_type=F32) + wbias

    def win_consume(g, sm):
        e = jnp.exp2(sm - jnp.max(sm, axis=0, keepdims=True))
        inv = 1.0 / jnp.sum(e, axis=0, keepdims=True)
        out_ref[:, gsl[g]] += (gate(g, 2) * inv) * _dot_t(vw, e.astype(MXU_DTYPE))

    staggered(win_scores, win_consume, 1)

    m_ref[...] = jnp.full(m_ref.shape, NEG_INF, F32)
    l_ref[...] = jnp.zeros(l_ref.shape, F32)
    acc_ref[...] = jnp.zeros(acc_ref.shape, F32)

    def issue_scores(kt, s_ref):
        k0 = pl.multiple_of(kt * tq, tq)
        kx = jnp.concatenate([ks_ref[pl.ds(k0, tq), :], e_ref[...]], axis=1)
        qx_ref[HEAD_DIM:HEAD_DIM + biasx_ref.shape[1], :] = biasx_ref[kt]
        for g in range(GROUP):
            s_ref[g] = jnp.dot(kx, qx_ref[:, gsl[g]], preferred_element_type=F32)

    def consume(kt, s_ref, diagonal):
        k0 = pl.multiple_of(kt * tq, tq)
        vv = vs_ref[pl.ds(k0, tq), :]
        if diagonal:
            causal = k0 + lax.broadcasted_iota(jnp.int32, (tq, 1), 0) <= t_q
        for g in range(GROUP):
            sm = jnp.where(causal, s_ref[g], NEG_INF) if diagonal else s_ref[g]
            m_old = m_ref[:, gsl[g]]
            m_new = jnp.maximum(m_old, jnp.max(sm, axis=0, keepdims=True))
            alpha = jnp.exp2(m_old - m_new)
            pt = jnp.exp2(sm - m_new)
            l_ref[:, gsl[g]] = alpha * l_ref[:, gsl[g]] + jnp.sum(pt, axis=0, keepdims=True)
            acc_ref[:, gsl[g]] = alpha * acc_ref[:, gsl[g]] + _dot_t(vv, pt.astype(MXU_DTYPE))
            m_ref[:, gsl[g]] = m_new

    def stage(kt, s_cur, s_next):
        issue_scores(kt + 1, s_next)
        consume(kt, s_cur, False)

    issue_scores(0, sa_ref)

    def pair(j, carry):
        stage(2 * j, sa_ref, sb_ref)
        stage(2 * j + 1, sb_ref, sa_ref)
        return carry

    lax.fori_loop(0, i // 2, pair, 0)

    @pl.when(i % 2 == 0)
    def _():
        consume(i, sa_ref, True)

    @pl.when(i % 2 == 1)
    def _():
        stage(i - 1, sa_ref, sb_ref)
        consume(i, sb_ref, True)

    for g in range(GROUP):
        out_t = out_ref[:, gsl[g]] + (gate(g, 1) / l_ref[:, gsl[g]]) * acc_ref[:, gsl[g]]
        o_ref[:, g * HEAD_DIM:(g + 1) * HEAD_DIM] = out_t.T.astype(o_ref.dtype)


def nsa_attention(q, k_cmp, v_cmp, k_sw, v_sw, gates_t, tq=256):
    t = q.shape[0]
    tq = min(tq, t)
    wl = min(WINDOW + tq, t)
    nc = k_cmp.shape[1]
    ns = t // SEL_BLOCK
    n_top = min(N_SELECT, ns)
    r = GROUP * tq
    lt = jnp.asarray(np.tril(np.ones((ns, ns))), MXU_DTYPE)
    cj = np.arange(nc)[None, :] * CMP_STRIDE
    si = np.arange(ns)[:, None] * SEL_BLOCK
    ov = np.clip(np.minimum(cj + CMP_BLOCK, si + SEL_BLOCK) - np.maximum(cj, si), 0, None) / CMP_BLOCK
    ov[:, (t - CMP_BLOCK) // CMP_STRIDE + 1:] = 0.0
    mt = jnp.asarray(ov, MXU_DTYPE)
    nb = tq // SEL_BLOCK
    onehot = jnp.asarray(np.arange(tq)[:, None] // SEL_BLOCK == np.arange(HEAD_DIM)[None, :], MXU_DTYPE)
    bias_rows = 16
    assert nb <= bias_rows and ns % nb == 0

    head = lambda off: pl.BlockSpec((t, HEAD_DIM), lambda h, i: (0, off + h))
    return pl.pallas_call(
        functools.partial(_nsa_kernel, tq=tq, n_top=n_top, wl=wl),
        grid=(N_KV_HEADS, t // tq),
        in_specs=[pl.BlockSpec((tq, GROUP * HEAD_DIM), lambda h, i: (i, h)),
                  pl.BlockSpec((1, nc, HEAD_DIM), lambda h, i: (h, 0, 0)),
                  pl.BlockSpec((1, nc, HEAD_DIM), lambda h, i: (h, 0, 0)),
                  head(0), head(N_KV_HEADS), head(0), head(N_KV_HEADS),
                  pl.BlockSpec((1, 16, tq), lambda h, i: (h, 0, i)),
                  pl.BlockSpec((ns, nc), lambda h, i: (0, 0)),
                  pl.BlockSpec((ns, ns), lambda h, i: (0, 0)),
                  pl.BlockSpec((tq, HEAD_DIM), lambda h, i: (0, 0))],
        out_specs=pl.BlockSpec((tq, GROUP * HEAD_DIM), lambda h, i: (i, h)),
        out_shape=jax.ShapeDtypeStruct((t, Q_W), MXU_DTYPE),
        scratch_shapes=[pltpu.VMEM((2 * HEAD_DIM, r), MXU_DTYPE),
                        pltpu.VMEM((ns // nb, bias_rows, r), MXU_DTYPE),
                        pltpu.VMEM((1, r), F32), pltpu.VMEM((1, r), F32),
                        pltpu.VMEM((HEAD_DIM, r), F32), pltpu.VMEM((HEAD_DIM, r), F32),
                        pltpu.VMEM((GROUP, tq, tq), F32), pltpu.VMEM((GROUP, tq, tq), F32)],
        compiler_params=_params(("parallel", "arbitrary"),
                                [((t, HEAD_DIM), MXU_DTYPE)] * 4 + [((nc, HEAD_DIM), MXU_DTYPE)] * 2
                                + [((tq, GROUP * HEAD_DIM), MXU_DTYPE)] * 2 + [((ns, nc), MXU_DTYPE)],
                                extra=12 * _nbytes((max(nc, wl), tq), F32)),
        name="nsa_attention",
    )(q, k_cmp, v_cmp, k_sw, k_sw, v_sw, v_sw, gates_t, mt, lt, onehot)


def _ffn(x, norm_g, w_gate, w_up, w_down):
    f = w_gate.shape[1]
    xn = rmsnorm_cast(x, norm_g)
    h = dual_mm(xn, Weight(w_gate), Weight(w_up), f, jax.nn.silu, MXU_DTYPE, "ffn_gate_up", tn=256)
    return mm_resid(h, w_down.astype(MXU_DTYPE), x, 0.5, 4, "ffn_down")


def _layer(x, p):
    t, d = x.shape
    x = _ffn(x, p["ffn1_norm"], p["ffn1_w_gate"], p["ffn1_w_up"], p["ffn1_w_down"])

    hn = rmsnorm_cast(x, p["mix_norm"])
    w_in = p["w_in"]
    conv_ch = p["conv_w"].shape[1]
    assert Q_W % KV_W == 0
    qb = Q_W // KV_W
    w_t = w_in.T
    qkv = Weight(w_t, 0)
    o = Q_W + 6 * KV_W
    assert o % 8 == 0 and NSA_GATE_W % 8 == 0 and conv_ch % 8 == 0 and w_t.shape[0] >= o + LANES
    w_gn = Weight(w_t, o); o += NSA_GATE_W
    w_ga = Weight(w_t, o); o += conv_ch
    w_gg = Weight(w_t, o); o += conv_ch
    w_m0 = Weight(w_t, o); o += d
    w_m1 = Weight(w_t, o); o += d

    scale = HEAD_DIM ** -0.5 * np.log2(np.e)
    q = mm(hn, qkv, Q_W, "headnorm", MXU_DTYPE, "proj_q", gain=jnp.tile(p["q_norm"] * scale, N_HEADS))
    k_sw = mm(hn, qkv, 2 * KV_W, "headnorm", MXU_DTYPE, "proj_k", block_of=lambda j: qb + 2 + 2 * j,
              gain=jnp.concatenate([jnp.tile(p["k_norm"][1], N_KV_HEADS), jnp.tile(p["k_norm"][2], N_KV_HEADS)]))
    v_sw = mm(hn, qkv, 2 * KV_W, "plain", MXU_DTYPE, "proj_v", block_of=lambda j: qb + 3 + 2 * j)
    kv_c = mm(hn, qkv, 2 * KV_W, "plain", F32, "proj_kv_cmp", block_of=lambda j: qb + j)
    g_nsa = mm(hn, w_gn, LANES, "sigmoid", F32, "proj_gates")
    u = dual_mm(hn, w_gg, w_ga, conv_ch, jax.nn.sigmoid, F32, "proj_glu", tn=256)

    chunks = kv_c.reshape(t // CMP_STRIDE, CMP_STRIDE, 2, N_KV_HEADS, HEAD_DIM)
    chunks = chunks.transpose(2, 3, 0, 1, 4).reshape(2, N_KV_HEADS, t // CMP_STRIDE, CMP_STRIDE * HEAD_DIM)
    k_cmp = compress(chunks[0], p["cmp_pos_k"], p["cmp_k_w1"], p["cmp_k_w2"], p["k_norm"][0], True, "compress_k")
    v_cmp = compress(chunks[1], p["cmp_pos_v"], p["cmp_v_w1"], p["cmp_v_w2"], p["k_norm"][0], False, "compress_v")

    gates_t = g_nsa[:, :NSA_GATE_W].reshape(t, N_KV_HEADS, GROUP * 3).transpose(1, 2, 0)
    gates_t = jnp.pad(gates_t, ((0, 0), (0, 16 - GROUP * 3), (0, 0)))
    attn = nsa_attention(q, k_cmp, v_cmp, k_sw, v_sw, gates_t)

    uc = conv_module(u, p["conv_w"], p["conv_b"], p["conv_ln_g"], p["conv_ln_b"])

    mix = gated_mm(attn, Weight(p["nsa_w_o"]), hn, w_m0, d, None, F32, "merge_nsa")
    mix = gated_mm(uc, Weight(p["conv_w_o"]), hn, w_m1, d, mix, MXU_DTYPE, "merge_conv")
    x = mm_resid(mix, p["w_out"], x, 1.0, 1, "mix_out", tn=512)

    return _ffn(x, p["ffn2_norm"], p["ffn2_w_gate"], p["ffn2_w_up"], p["ffn2_w_down"])


_NAMES = ("ffn1_norm", "ffn1_w_gate", "ffn1_w_up", "ffn1_w_down", "mix_norm", "w_in", "q_norm", "k_norm",
          "cmp_pos_k", "cmp_k_w1", "cmp_k_w2", "cmp_pos_v", "cmp_v_w1", "cmp_v_w2", "nsa_w_o",
          "conv_w", "conv_b", "conv_ln_g", "conv_ln_b", "conv_w_o", "w_out",
          "ffn2_norm", "ffn2_w_gate", "ffn2_w_up", "ffn2_w_down")


def kernel(x, ffn1_norm, ffn1_w_gate, ffn1_w_up, ffn1_w_down, mix_norm, w_in, q_norm, k_norm, cmp_pos_k, cmp_k_w1, cmp_k_w2, cmp_pos_v, cmp_v_w1, cmp_v_w2, nsa_w_o, conv_w, conv_b, conv_ln_g, conv_ln_b, conv_w_o, w_out, ffn2_norm, ffn2_w_gate, ffn2_w_up, ffn2_w_down):
    params = (ffn1_norm, ffn1_w_gate, ffn1_w_up, ffn1_w_down, mix_norm, w_in, q_norm, k_norm, cmp_pos_k,
              cmp_k_w1, cmp_k_w2, cmp_pos_v, cmp_v_w1, cmp_v_w2, nsa_w_o, conv_w, conv_b, conv_ln_g,
              conv_ln_b, conv_w_o, w_out, ffn2_norm, ffn2_w_gate, ffn2_w_up, ffn2_w_down)
    b, t, d = x.shape
    depth = ffn1_norm.shape[0]
    xs = x.reshape(b * t, d)
    outs = []
    for bi in range(b):
        xb = xs[bi * t:(bi + 1) * t]
        for l in range(depth):
            xb = _layer(xb, {n: a[l] for n, a in zip(_NAMES, params)})
        outs.append(xb)
    out = outs[0] if b == 1 else jnp.concatenate(outs, axis=0)
    return out.reshape(b, t, d)
```

```python
import functools
from typing import NamedTuple, Optional

import numpy as np
import jax
import jax.numpy as jnp
from jax import lax
from jax.experimental import pallas as pl
from jax.experimental.pallas import tpu as pltpu

HEAD_DIM = 128
N_HEADS = 16
N_KV_HEADS = 4
GROUP = N_HEADS // N_KV_HEADS
CMP_BLOCK = 32
CMP_STRIDE = 16
SEL_BLOCK = 64
N_SELECT = 16
WINDOW = 512
CONV_WIDTH = 31
NEG_INF = -1e30
FORCE = 1e9
EPS = 1e-6

Q_W = N_HEADS * HEAD_DIM
KV_W = N_KV_HEADS * HEAD_DIM
NSA_GATE_W = N_HEADS * 3

MXU_DTYPE = jnp.bfloat16
LANES = 128
V7X_VMEM_BYTES = 64 * 1024 * 1024
VMEM_CAP = V7X_VMEM_BYTES - 6 * 1024 * 1024

F32 = jnp.float32


def _nbytes(shape, dtype):
    return int(np.prod(shape)) * jnp.dtype(dtype).itemsize


def _params(semantics, blocks, extra=0):
    need = 2 * sum(_nbytes(s, d) for s, d in blocks) + extra + (2 << 20)
    return pltpu.CompilerParams(dimension_semantics=semantics,
                                vmem_limit_bytes=int(min(max(need, 16 << 20), VMEM_CAP)))


def _tile(n, pref):
    return pref if n % pref == 0 else n


def _rmsnorm_kernel(x_ref, g_ref, o_ref):
    x = x_ref[...]
    ms = jnp.mean(x * x, axis=-1, keepdims=True)
    o_ref[...] = (x * lax.rsqrt(ms + EPS) * g_ref[...]).astype(o_ref.dtype)


def rmsnorm_cast(x, gain):
    m, d = x.shape
    tm = _tile(m, 256)
    return pl.pallas_call(
        _rmsnorm_kernel,
        grid=(m // tm,),
        in_specs=[pl.BlockSpec((tm, d), lambda i: (i, 0)),
                  pl.BlockSpec((1, d), lambda i: (0, 0))],
        out_specs=pl.BlockSpec((tm, d), lambda i: (i, 0)),
        out_shape=jax.ShapeDtypeStruct((m, d), MXU_DTYPE),
        compiler_params=_params(("parallel",), [((tm, d), F32), ((tm, d), MXU_DTYPE)],
                                extra=2 * _nbytes((tm, d), F32)),
        name="rmsnorm_cast",
    )(x, gain.reshape(1, d).astype(F32))


class Weight(NamedTuple):
    arr: jax.Array
    row0: Optional[int] = None


def _w_spec(w, k, tn, block_of):
    if w.row0 is None:
        return pl.BlockSpec((k, tn), lambda i, j: (0, block_of(j)))
    assert w.row0 % 8 == 0 and tn % 8 == 0
    return pl.BlockSpec((pl.Element(tn), pl.Element(k)),
                        lambda i, j: (pl.multiple_of(w.row0 + block_of(j) * tn, 8), 0))


def _w_block(w, k, tn):
    return ((k, tn) if w.row0 is None else (tn, k)), w.arr.dtype


def _mxu(x, w_ref, transposed):
    w = w_ref[...].astype(x.dtype)
    if transposed:
        return lax.dot_general(x, w, (((1,), (1,)), ((), ())), preferred_element_type=F32)
    return jnp.dot(x, w, preferred_element_type=F32)


def _dual_kernel(*refs, act, transposed, has_cast):
    if has_cast:
        x_ref, w1_ref, w2_ref, c_ref, o_ref, co_ref = refs
        co_ref[...] = c_ref[...].astype(co_ref.dtype)
    else:
        x_ref, w1_ref, w2_ref, o_ref = refs
    x = x_ref[...]
    a = _mxu(x, w1_ref, transposed)
    b = _mxu(x, w2_ref, transposed)
    o_ref[...] = (act(a) * b).astype(o_ref.dtype)


def dual_mm(x, w1, w2, n, act, out_dtype, name, tn=512, cast=None):
    m, k = x.shape
    tm, tn = _tile(m, 1024), _tile(n, tn)
    assert n % tn == 0 and (w1.row0 is None) == (w2.row0 is None)
    block_of = lambda j: j
    ni, nj = m // tm, n // tn
    in_specs = [pl.BlockSpec((tm, k), lambda i, j: (i, 0)),
                _w_spec(w1, k, tn, block_of),
                _w_spec(w2, k, tn, block_of)]
    out_specs = pl.BlockSpec((tm, tn), lambda i, j: (i, j))
    out_shape = jax.ShapeDtypeStruct((m, n), out_dtype)
    blocks = [((tm, k), x.dtype), _w_block(w1, k, tn), _w_block(w2, k, tn), ((tm, tn), out_dtype)]
    args = [x, w1.arr, w2.arr]
    if cast is not None:
        rows, cols = cast.shape
        slab = rows // (ni * nj)
        assert slab * ni * nj == rows and slab % 16 == 0
        slab_spec = pl.BlockSpec((slab, cols), lambda i, j: (i * nj + j, 0))
        in_specs.append(slab_spec)
        out_specs = [out_specs, slab_spec]
        out_shape = [out_shape, jax.ShapeDtypeStruct((rows, cols), MXU_DTYPE)]
        blocks += [((slab, cols), cast.dtype), ((slab, cols), MXU_DTYPE)]
        args.append(cast)
    return pl.pallas_call(
        functools.partial(_dual_kernel, act=act, transposed=w1.row0 is not None, has_cast=cast is not None),
        grid=(ni, nj),
        in_specs=in_specs,
        out_specs=out_specs,
        out_shape=out_shape,
        compiler_params=_params(("parallel", "arbitrary"), blocks, extra=4 * _nbytes((tm, tn), F32)),
        name=name,
    )(*args)


def _mm_kernel(x_ref, w_ref, g_ref, o_ref, *, mode, transposed):
    acc = _mxu(x_ref[...], w_ref, transposed)
    if mode == "headnorm":
        for c in range(acc.shape[1] // HEAD_DIM):
            sl = slice(c * HEAD_DIM, (c + 1) * HEAD_DIM)
            v = acc[:, sl]
            ms = jnp.mean(v * v, axis=-1, keepdims=True)
            o_ref[:, sl] = (v * lax.rsqrt(ms + EPS) * g_ref[:, sl]).astype(o_ref.dtype)
    elif mode == "sigmoid":
        o_ref[...] = jax.nn.sigmoid(acc).astype(o_ref.dtype)
    else:
        o_ref[...] = acc.astype(o_ref.dtype)


def mm(x, w, n, mode, out_dtype, name, gain=None, block_of=None):
    m, k = x.shape
    tm, tn = _tile(m, 1024), _tile(n, 512)
    if gain is None:
        gain = jnp.ones((1, n), F32)
    if block_of is None:
        block_of = lambda j: j
    return pl.pallas_call(
        functools.partial(_mm_kernel, mode=mode, transposed=w.row0 is not None),
        grid=(m // tm, n // tn),
        in_specs=[pl.BlockSpec((tm, k), lambda i, j: (i, 0)),
                  _w_spec(w, k, tn, block_of),
                  pl.BlockSpec((1, tn), lambda i, j: (0, j))],
        out_specs=pl.BlockSpec((tm, tn), lambda i, j: (i, j)),
        out_shape=jax.ShapeDtypeStruct((m, n), out_dtype),
        compiler_params=_params(("parallel", "arbitrary"),
                                [((tm, k), x.dtype), _w_block(w, k, tn), ((tm, tn), out_dtype)],
                                extra=3 * _nbytes((tm, tn), F32)),
        name=name,
    )(x, w.arr, gain.reshape(1, n).astype(F32))


def _mm_acc_kernel(*refs, scale, nk, has_tail):
    if has_tail:
        h_ref, w_ref, ht_ref, wt_ref, r_ref, o_ref, acc_ref = refs
    else:
        h_ref, w_ref, r_ref, o_ref, acc_ref = refs
    kk = pl.program_id(2)

    @pl.when(kk == 0)
    def _():
        if has_tail:
            acc_ref[...] = _mxu(ht_ref[...], wt_ref, False)
        else:
            acc_ref[...] = jnp.zeros_like(acc_ref)

    acc_ref[...] += _mxu(h_ref[...], w_ref, False)

    @pl.when(kk == nk - 1)
    def _():
        o_ref[...] = (r_ref[...] + scale * acc_ref[...]).astype(o_ref.dtype)


def _k_tiling(k, nk):
    if k % (nk * LANES) == 0:
        return k // nk, nk, 0
    tk = k // nk // LANES * LANES
    tail = k - nk * tk
    if tk > 0 and tail % LANES == 0 and (nk * tk) % tail == 0:
        return tk, nk, tail
    return k, 1, 0


def mm_resid(h, w, resid, scale, nk, name, tn=1024):
    m, k = h.shape
    n = w.shape[1]
    tm, tn = _tile(m, 1024), _tile(n, tn)
    tk, nk, tail = _k_tiling(k, nk)
    in_specs = [pl.BlockSpec((tm, tk), lambda i, j, kk: (i, kk)),
                pl.BlockSpec((tk, tn), lambda i, j, kk: (kk, j))]
    blocks = [((tm, tk), h.dtype), ((tk, tn), w.dtype), ((tm, tn), F32), ((tm, tn), F32)]
    args = [h, w]
    if tail:
        tb = nk * tk // tail
        in_specs += [pl.BlockSpec((tm, tail), lambda i, j, kk: (i, tb)),
                     pl.BlockSpec((tail, tn), lambda i, j, kk: (tb, j))]
        blocks += [((tm, tail), h.dtype), ((tail, tn), w.dtype)]
        args += [h, w]
    in_specs.append(pl.BlockSpec((tm, tn), lambda i, j, kk: (i, j)))
    return pl.pallas_call(
        functools.partial(_mm_acc_kernel, scale=scale, nk=nk, has_tail=bool(tail)),
        grid=(m // tm, n // tn, nk),
        in_specs=in_specs,
        out_specs=pl.BlockSpec((tm, tn), lambda i, j, kk: (i, j)),
        out_shape=jax.ShapeDtypeStruct((m, n), F32),
        scratch_shapes=[pltpu.VMEM((tm, tn), F32)],
        compiler_params=_params(("parallel", "arbitrary", "arbitrary"), blocks,
                                extra=3 * _nbytes((tm, tn), F32)),
        name=name,
    )(*args, resid)


def _gated_kernel(*refs, has_prev, wo_transposed, wm_transposed):
    if has_prev:
        a_ref, wo_ref, h_ref, wm_ref, p_ref, o_ref = refs
    else:
        a_ref, wo_ref, h_ref, wm_ref, o_ref = refs
    y = _mxu(a_ref[...], wo_ref, wo_transposed)
    g = jax.nn.sigmoid(_mxu(h_ref[...], wm_ref, wm_transposed))
    r = g * y
    if has_prev:
        r = p_ref[...] + r
    o_ref[...] = r.astype(o_ref.dtype)


def gated_mm(a, wo, h, wm, n, prev, out_dtype, name):
    m, ka = a.shape
    kh = h.shape[1]
    tm, tn = _tile(m, 1024), _tile(n, 256)
    has_prev = prev is not None
    ident = lambda j: j
    in_specs = [pl.BlockSpec((tm, ka), lambda i, j: (i, 0)),
                _w_spec(wo, ka, tn, ident),
                pl.BlockSpec((tm, kh), lambda i, j: (i, 0)),
                _w_spec(wm, kh, tn, ident)]
    blocks = [((tm, ka), a.dtype), _w_block(wo, ka, tn), ((tm, kh), h.dtype), _w_block(wm, kh, tn),
              ((tm, tn), out_dtype)]
    args = [a, wo.arr, h, wm.arr]
    if has_prev:
        in_specs.append(pl.BlockSpec((tm, tn), lambda i, j: (i, j)))
        blocks.append(((tm, tn), prev.dtype))
        args.append(prev)
    return pl.pallas_call(
        functools.partial(_gated_kernel, has_prev=has_prev, wo_transposed=wo.row0 is not None,
                          wm_transposed=wm.row0 is not None),
        grid=(m // tm, n // tn),
        in_specs=in_specs,
        out_specs=pl.BlockSpec((tm, tn), lambda i, j: (i, j)),
        out_shape=jax.ShapeDtypeStruct((m, n), out_dtype),
        compiler_params=_params(("parallel", "arbitrary"), blocks, extra=4 * _nbytes((tm, tn), F32)),
        name=name,
    )(*args)


def _gelu_tanh(x):
    return 0.5 * x * (1.0 + jnp.tanh(np.sqrt(2.0 / np.pi).astype(np.float32) * (x + 0.044715 * (x * x * x))))


def _compress_kernel(a_ref, pa_ref, pb_ref, w1a_ref, w1b_ref, w2_ref, g_ref, o_ref, *, norm):
    nch = a_ref.shape[0] // CMP_STRIDE
    a = jnp.concatenate([a_ref[pl.ds(tok, nch, stride=CMP_STRIDE), :] for tok in range(CMP_STRIDE)], axis=1)
    ua = jnp.dot((a + pa_ref[...]).astype(MXU_DTYPE), w1a_ref[...], preferred_element_type=F32)
    ub = jnp.dot((a + pb_ref[...]).astype(MXU_DTYPE), w1b_ref[...], preferred_element_type=F32)
    pre = ua + pltpu.roll(ub, nch - 1, 0)
    hid = _gelu_tanh(pre).astype(MXU_DTYPE)
    out = jnp.dot(hid, w2_ref[...], preferred_element_type=F32)
    if norm:
        ms = jnp.mean(out * out, axis=-1, keepdims=True)
        out = out * lax.rsqrt(ms + EPS) * g_ref[...]
    o_ref[0] = out.astype(o_ref.dtype)


def compress(kv, head0, pos, w1, w2, gain, norm, name):
    t = kv.shape[0]
    hkv, nch, cw = N_KV_HEADS, t // CMP_STRIDE, CMP_STRIDE * HEAD_DIM
    hid = w1.shape[1]
    posf = pos.reshape(1, CMP_BLOCK * HEAD_DIM).astype(F32)
    pa, pb = posf[:, :cw], posf[:, cw:]
    w1c = w1.astype(MXU_DTYPE)
    w1a, w1b = w1c[:cw], w1c[cw:]
    full = lambda shape: pl.BlockSpec(shape, lambda h: (0,) * len(shape))
    return pl.pallas_call(
        functools.partial(_compress_kernel, norm=norm),
        grid=(hkv,),
        in_specs=[pl.BlockSpec((t, HEAD_DIM), lambda h: (0, head0 + h)),
                  full((1, cw)), full((1, cw)), full((cw, hid)), full((cw, hid)),
                  full((hid, HEAD_DIM)), full((1, HEAD_DIM))],
        out_specs=pl.BlockSpec((1, nch, HEAD_DIM), lambda h: (h, 0, 0)),
        out_shape=jax.ShapeDtypeStruct((hkv, nch, HEAD_DIM), MXU_DTYPE),
        compiler_params=_params(("parallel",),
                                [((nch, cw), F32), ((cw, hid), MXU_DTYPE), ((cw, hid), MXU_DTYPE)],
                                extra=8 * _nbytes((nch, cw), F32)),
        name=name,
    )(kv, pa, pb, w1a, w1b, w2.astype(MXU_DTYPE), gain.reshape(1, HEAD_DIM).astype(F32))


def _conv_kernel(cur_ref, prev_ref, cw_ref, cb_ref, lg_ref, lb_ref, o_ref, ext_ref, xs_ref, y_ref, *,
                 halo, rc, cc):
    i = pl.program_id(0)
    tm, ch = cur_ref.shape
    sub = 8
    ext_ref[0:halo, :] = jnp.where(i > 0, prev_ref[tm - halo:tm, :], 0.0)
    ext_ref[halo:halo + tm, :] = cur_ref[...]
    base = halo - (CONV_WIDTH - 1)
    span = xs_ref.shape[1]
    for c0 in range(0, ch, cc):
        for r in range(1, sub):
            xs_ref[r - 1] = ext_ref[r:r + span, c0:c0 + cc]
        for r0 in range(0, tm, rc):
            acc = jnp.zeros((rc, cc), F32) + cb_ref[:, c0:c0 + cc]
            for w in range(CONV_WIDTH):
                a, r = divmod(base + r0 + w, sub)
                if r == 0:
                    tap = ext_ref[a * sub:a * sub + rc, c0:c0 + cc]
                else:
                    tap = xs_ref[r - 1, a * sub:a * sub + rc, :]
                acc = acc + tap * cw_ref[w:w + 1, c0:c0 + cc]
            y_ref[r0:r0 + rc, c0:c0 + cc] = acc
    y = y_ref[...]
    mu = jnp.mean(y, axis=-1, keepdims=True)
    d = y - mu
    var = jnp.mean(d * d, axis=-1, keepdims=True)
    z = d * lax.rsqrt(var + EPS) * lg_ref[...] + lb_ref[...]
    o_ref[...] = (z * jax.nn.sigmoid(z)).astype(o_ref.dtype)


def conv_module(u, conv_w, conv_b, ln_g, ln_b):
    t, ch = u.shape
    tm = _tile(t, 256)
    halo = 32
    row = lambda v: v.reshape(1, ch).astype(F32)
    full = lambda shape: pl.BlockSpec(shape, lambda i: (0, 0))
    cc = min(512, ch)
    return pl.pallas_call(
        functools.partial(_conv_kernel, halo=halo, rc=64, cc=cc),
        grid=(t // tm,),
        in_specs=[pl.BlockSpec((tm, ch), lambda i: (i, 0)),
                  pl.BlockSpec((tm, ch), lambda i: (jnp.maximum(i - 1, 0), 0)),
                  full((CONV_WIDTH, ch)), full((1, ch)), full((1, ch)), full((1, ch))],
        out_specs=pl.BlockSpec((tm, ch), lambda i: (i, 0)),
        out_shape=jax.ShapeDtypeStruct((t, ch), MXU_DTYPE),
        scratch_shapes=[pltpu.VMEM((tm + halo, ch), F32), pltpu.VMEM((7, tm + halo - 8, cc), F32),
                        pltpu.VMEM((tm, ch), F32)],
        compiler_params=_params(("parallel",),
                                [((tm, ch), F32), ((tm, ch), F32), ((tm, ch), MXU_DTYPE)],
                                extra=6 * _nbytes((tm, ch), F32)),
        name="conv_module",
    )(u, u, conv_w.astype(F32), row(conv_b), row(ln_g), row(ln_b))


def _dot_t(a, b):
    return lax.dot_general(a, b, (((0,), (0,)), ((), ())), preferred_element_type=F32)


def _sort_desc(c):
    n = len(c)
    if n == 1:
        return list(c)
    return _merge_desc(_sort_desc(c[:n // 2]) + _sort_desc(c[n // 2:])[::-1])


def _merge_desc(c):
    c = list(c)
    n = len(c)
    d = n // 2
    while d >= 1:
        for a in range(n):
            if a & d == 0:
                c[a], c[a + d] = jnp.maximum(c[a], c[a + d]), jnp.minimum(c[a], c[a + d])
        d //= 2
    return c


def _kth_largest(score, k):
    ns = score.shape[0]
    sub = 8
    a = _sort_desc([score[j * sub:(j + 1) * sub, :] for j in range(ns // sub)])[:k]
    shift = sub // 2
    while True:
        b = [pltpu.roll(v, shift, 0) for v in a]
        if 2 * len(a) <= k:
            top = a + b[::-1]
        else:
            top = [jnp.maximum(a[j], b[len(a) - 1 - j]) for j in range(len(a))]
        if shift == 1:
            break
        a = _merge_desc(top)
        shift //= 2
    kth = top[0]
    for v in top[1:]:
        kth = jnp.minimum(kth, v)
    return kth[0:1, :]


def _nsa_kernel(q_ref, kc_ref, vc_ref, ks_ref, kw_ref, vs_ref, vw_ref, gt_ref, mt_ref, lt_ref, e_ref, o_ref,
                qx_ref, biasx_ref, m_ref, l_ref, acc_ref, out_ref, sa_ref, sb_ref, *, tq, n_top, wl):
    i = pl.program_id(1)
    t0 = i * tq
    nc = kc_ref.shape[1]
    ns = mt_ref.shape[0]
    nb = tq // SEL_BLOCK
    gsl = [slice(g * tq, (g + 1) * tq) for g in range(GROUP)]

    def gate(g, b):
        return gt_ref[0, g * 3 + b:g * 3 + b + 1, :]

    def staggered(scores, consume, ahead):
        pending = [scores(g) for g in range(min(ahead, GROUP))]
        for g in range(GROUP):
            if g + ahead < GROUP:
                pending.append(scores(g + ahead))
            consume(g, pending[g])

    for g in range(GROUP):
        qg = q_ref[:, g * HEAD_DIM:(g + 1) * HEAD_DIM].astype(F32)
        qx_ref[0:HEAD_DIM, gsl[g]] = qg.T.astype(MXU_DTYPE)
    qx_ref[HEAD_DIM:, :] = jnp.zeros((HEAD_DIM, GROUP * tq), MXU_DTYPE)
    t_q = t0 + lax.broadcasted_iota(jnp.int32, (1, tq), 1)

    cmp_end = lax.broadcasted_iota(jnp.int32, (nc, 1), 0) * CMP_STRIDE + (CMP_BLOCK - 1)
    cbias = jnp.where(cmp_end <= t_q, 0.0, NEG_INF)
    kc = kc_ref[0]
    vc = vc_ref[0]
    psums = []

    def cmp_scores(g):
        return jnp.dot(kc, qx_ref[0:HEAD_DIM, gsl[g]], preferred_element_type=F32) + cbias

    def cmp_consume(g, sm):
        mx = jnp.maximum(jnp.max(sm, axis=0, keepdims=True), 0.5 * NEG_INF)
        e = jnp.exp2(sm - mx)
        den = jnp.sum(e, axis=0, keepdims=True)
        p = e * (1.0 / jnp.where(den > 0.0, den, 1.0))
        out_ref[:, gsl[g]] = gate(g, 0) * _dot_t(vc, p.astype(MXU_DTYPE))
        psums.append(p)

    staggered(cmp_scores, cmp_consume, 1)
    psum = functools.reduce(lambda a, b: a + b, psums)

    p_hi = psum.astype(MXU_DTYPE)
    p_lo = (psum - p_hi.astype(F32)).astype(MXU_DTYPE)
    imp = (jnp.dot(mt_ref[...], p_hi, preferred_element_type=F32)
           + jnp.dot(mt_ref[...], p_lo, preferred_element_type=F32))
    s_id = lax.broadcasted_iota(jnp.int32, (ns, 1), 0)
    cur = t_q // SEL_BLOCK
    forced = (s_id == 0) | (s_id == cur) | (s_id == cur - 1)
    valid = s_id * SEL_BLOCK <= t_q
    score = jnp.where(forced, FORCE, jnp.where(valid, imp, NEG_INF))
    kth = _kth_largest(score, n_top)
    above = score > kth
    equal = score == kth
    n_above = jnp.sum(jnp.where(above, 1.0, 0.0), axis=0, keepdims=True)
    n_equal_upto = jnp.dot(lt_ref[...], jnp.where(equal, 1.0, 0.0).astype(MXU_DTYPE),
                           preferred_element_type=F32)
    picked = above | (equal & (n_equal_upto <= n_top - n_above))
    block_bias = jnp.where(picked, 0.0, NEG_INF)
    pad_rows = jnp.zeros((biasx_ref.shape[1] - nb, tq), F32)
    for j in range(ns // nb):
        slab = jnp.concatenate([block_bias[j * nb:(j + 1) * nb, :], pad_rows], axis=0)
        biasx_ref[j] = jnp.concatenate([slab] * GROUP, axis=1).astype(MXU_DTYPE)

    ws = pl.multiple_of(jnp.maximum(t0 + tq - wl, 0), SEL_BLOCK)
    back = t_q - (ws + lax.broadcasted_iota(jnp.int32, (wl, 1), 0))
    wbias = jnp.where(lax.bitcast_convert_type(back, jnp.uint32) < WINDOW, 0.0, NEG_INF)
    kw = kw_ref[pl.ds(ws, wl), :]
    vw = vw_ref[pl.ds(ws, wl), :]

    def win_scores(g):
        return jnp.dot(kw, qx_ref[0:HEAD_DIM, gsl[g]], preferred_element_type=F32) + wbias

    def win_consume(g, sm):
        e = jnp.exp2(sm - jnp.max(sm, axis=0, keepdims=True))
        inv = 1.0 / jnp.sum(e, axis=0, keepdims=True)
        out_ref[:, gsl[g]] += (gate(g, 2) * inv) * _dot_t(vw, e.astype(MXU_DTYPE))

    staggered(win_scores, win_consume, 1)

    m_ref[...] = jnp.full(m_ref.shape, NEG_INF, F32)
    l_ref[...] = jnp.zeros(l_ref.shape, F32)
    acc_ref[...] = jnp.zeros(acc_ref.shape, F32)

    def issue_scores(kt, s_ref):
        k0 = pl.multiple_of(kt * tq, tq)
        kx = jnp.concatenate([ks_ref[pl.ds(k0, tq), :], e_ref[...]], axis=1)
        qx_ref[HEAD_DIM:HEAD_DIM + biasx_ref.shape[1], :] = biasx_ref[kt]
        for g in range(GROUP):
            s_ref[g] = jnp.dot(kx, qx_ref[:, gsl[g]], preferred_element_type=F32)

    def consume(kt, s_ref, diagonal):
        k0 = pl.multiple_of(kt * tq, tq)
        vv = vs_ref[pl.ds(k0, tq), :]
        if diagonal:
            causal = k0 + lax.broadcasted_iota(jnp.int32, (tq, 1), 0) <= t_q
        for g in range(GROUP):
            sm = jnp.where(causal, s_ref[g], NEG_INF) if diagonal else s_ref[g]
            m_old = m_ref[:, gsl[g]]
            m_new = jnp.maximum(m_old, jnp.max(sm, axis=0, keepdims=True))
            alpha = jnp.exp2(m_old - m_new)
            pt = jnp.exp2(sm - m_new)
            l_ref[:, gsl[g]] = alpha * l_ref[:, gsl[g]] + jnp.sum(pt, axis=0, keepdims=True)
            acc_ref[:, gsl[g]] = alpha * acc_ref[:, gsl[g]] + _dot_t(vv, pt.astype(MXU_DTYPE))
            m_ref[:, gsl[g]] = m_new

    def stage(kt, s_cur, s_next):
        issue_scores(kt + 1, s_next)
        consume(kt, s_cur, False)

    issue_scores(0, sa_ref)

    def pair(j, carry):
        stage(2 * j, sa_ref, sb_ref)
        stage(2 * j + 1, sb_ref, sa_ref)
        return carry

    lax.fori_loop(0, i // 2, pair, 0)

    @pl.when(i % 2 == 0)
    def _():
        consume(i, sa_ref, True)

    @pl.when(i % 2 == 1)
    def _():
        stage(i - 1, sa_ref, sb_ref)
        consume(i, sb_ref, True)

    for g in range(GROUP):
        out_t = out_ref[:, gsl[g]] + (gate(g, 1) / l_ref[:, gsl[g]]) * acc_ref[:, gsl[g]]
        o_ref[:, g * HEAD_DIM:(g + 1) * HEAD_DIM] = out_t.T.astype(o_ref.dtype)


def nsa_attention(q, k_cmp, v_cmp, k_sw, v_sw, gates_t, tq=256):
    t = q.shape[0]
    tq = min(tq, t)
    wl = min(WINDOW + tq, t)
    nc = k_cmp.shape[1]
    ns = t // SEL_BLOCK
    n_top = min(N_SELECT, ns)
    r = GROUP * tq
    lt = jnp.asarray(np.tril(np.ones((ns, ns))), MXU_DTYPE)
    cj = np.arange(nc)[None, :] * CMP_STRIDE
    si = np.arange(ns)[:, None] * SEL_BLOCK
    ov = np.clip(np.minimum(cj + CMP_BLOCK, si + SEL_BLOCK) - np.maximum(cj, si), 0, None) / CMP_BLOCK
    ov[:, (t - CMP_BLOCK) // CMP_STRIDE + 1:] = 0.0
    mt = jnp.asarray(ov, MXU_DTYPE)
    nb = tq // SEL_BLOCK
    onehot = jnp.asarray(np.arange(tq)[:, None] // SEL_BLOCK == np.arange(HEAD_DIM)[None, :], MXU_DTYPE)
    bias_rows = 16
    assert nb <= bias_rows and ns % nb == 0

    head = lambda off: pl.BlockSpec((t, HEAD_DIM), lambda h, i: (0, off + h))
    return pl.pallas_call(
        functools.partial(_nsa_kernel, tq=tq, n_top=n_top, wl=wl),
        grid=(N_KV_HEADS, t // tq),
        in_specs=[pl.BlockSpec((tq, GROUP * HEAD_DIM), lambda h, i: (i, h)),
                  pl.BlockSpec((1, nc, HEAD_DIM), lambda h, i: (h, 0, 0)),
                  pl.BlockSpec((1, nc, HEAD_DIM), lambda h, i: (h, 0, 0)),
                  head(0), head(N_KV_HEADS), head(0), head(N_KV_HEADS),
                  pl.BlockSpec((1, 16, tq), lambda h, i: (h, 0, i)),
                  pl.BlockSpec((ns, nc), lambda h, i: (0, 0)),
                  pl.BlockSpec((ns, ns), lambda h, i: (0, 0)),
                  pl.BlockSpec((tq, HEAD_DIM), lambda h, i: (0, 0))],
        out_specs=pl.BlockSpec((tq, GROUP * HEAD_DIM), lambda h, i: (i, h)),
        out_shape=jax.ShapeDtypeStruct((t, Q_W), MXU_DTYPE),
        scratch_shapes=[pltpu.VMEM((2 * HEAD_DIM, r), MXU_DTYPE),
                        pltpu.VMEM((ns // nb, bias_rows, r), MXU_DTYPE),
                        pltpu.VMEM((1, r), F32), pltpu.VMEM((1, r), F32),
                        pltpu.VMEM((HEAD_DIM, r), F32), pltpu.VMEM((HEAD_DIM, r), F32),
                        pltpu.VMEM((GROUP, tq, tq), F32), pltpu.VMEM((GROUP, tq, tq), F32)],
        compiler_params=_params(("parallel", "arbitrary"),
                                [((t, HEAD_DIM), MXU_DTYPE)] * 4 + [((nc, HEAD_DIM), MXU_DTYPE)] * 2
                                + [((tq, GROUP * HEAD_DIM), MXU_DTYPE)] * 2 + [((ns, nc), MXU_DTYPE)],
                                extra=12 * _nbytes((max(nc, wl), tq), F32)),
        name="nsa_attention",
    )(q, k_cmp, v_cmp, k_sw, k_sw, v_sw, v_sw, gates_t, mt, lt, onehot)


def _ffn(x, norm_g, w_gate, w_up, w_down):
    f = w_gate.shape[1]
    xn = rmsnorm_cast(x, norm_g)
    h, wd = dual_mm(xn, Weight(w_gate), Weight(w_up), f, jax.nn.silu, MXU_DTYPE, "ffn_gate_up", tn=256,
                    cast=w_down)
    return mm_resid(h, wd, x, 0.5, 4, "ffn_down")


def _layer(x, p):
    t, d = x.shape
    x = _ffn(x, p["ffn1_norm"], p["ffn1_w_gate"], p["ffn1_w_up"], p["ffn1_w_down"])

    hn = rmsnorm_cast(x, p["mix_norm"])
    w_in = p["w_in"]
    conv_ch = p["conv_w"].shape[1]
    assert Q_W % KV_W == 0
    qb = Q_W // KV_W
    w_t = w_in.T
    qkv = Weight(w_t, 0)
    o = Q_W + 6 * KV_W
    assert o % 8 == 0 and NSA_GATE_W % 8 == 0 and conv_ch % 8 == 0 and w_t.shape[0] >= o + LANES
    w_gn = Weight(w_t, o); o += NSA_GATE_W
    w_ga = Weight(w_t, o); o += conv_ch
    w_gg = Weight(w_t, o); o += conv_ch
    w_m0 = Weight(w_t, o); o += d
    w_m1 = Weight(w_t, o); o += d

    scale = HEAD_DIM ** -0.5 * np.log2(np.e)
    q = mm(hn, qkv, Q_W, "headnorm", MXU_DTYPE, "proj_q", gain=jnp.tile(p["q_norm"] * scale, N_HEADS))
    k_sw = mm(hn, qkv, 2 * KV_W, "headnorm", MXU_DTYPE, "proj_k", block_of=lambda j: qb + 2 + 2 * j,
              gain=jnp.concatenate([jnp.tile(p["k_norm"][1], N_KV_HEADS), jnp.tile(p["k_norm"][2], N_KV_HEADS)]))
    v_sw = mm(hn, qkv, 2 * KV_W, "plain", MXU_DTYPE, "proj_v", block_of=lambda j: qb + 3 + 2 * j)
    kv_c = mm(hn, qkv, 2 * KV_W, "plain", F32, "proj_kv_cmp", block_of=lambda j: qb + j)
    g_nsa = mm(hn, w_gn, LANES, "sigmoid", F32, "proj_gates")
    u = dual_mm(hn, w_gg, w_ga, conv_ch, jax.nn.sigmoid, F32, "proj_glu", tn=256)

    k_cmp = compress(kv_c, 0, p["cmp_pos_k"], p["cmp_k_w1"], p["cmp_k_w2"], p["k_norm"][0], True, "compress_k")
    v_cmp = compress(kv_c, N_KV_HEADS, p["cmp_pos_v"], p["cmp_v_w1"], p["cmp_v_w2"], p["k_norm"][0], False,
                     "compress_v")

    gates_t = g_nsa[:, :NSA_GATE_W].reshape(t, N_KV_HEADS, GROUP * 3).transpose(1, 2, 0)
    gates_t = jnp.pad(gates_t, ((0, 0), (0, 16 - GROUP * 3), (0, 0)))
    attn = nsa_attention(q, k_cmp, v_cmp, k_sw, v_sw, gates_t)

    uc = conv_module(u, p["conv_w"], p["conv_b"], p["conv_ln_g"], p["conv_ln_b"])

    mix = gated_mm(attn, Weight(p["nsa_w_o"]), hn, w_m0, d, None, F32, "merge_nsa")
    mix = gated_mm(uc, Weight(p["conv_w_o"]), hn, w_m1, d, mix, MXU_DTYPE, "merge_conv")
    x = mm_resid(mix, p["w_out"], x, 1.0, 1, "mix_out", tn=512)

    return _ffn(x, p["ffn2_norm"], p["ffn2_w_gate"], p["ffn2_w_up"], p["ffn2_w_down"])


_NAMES = ("ffn1_norm", "ffn1_w_gate", "ffn1_w_up", "ffn1_w_down", "mix_norm", "w_in", "q_norm", "k_norm",
          "cmp_pos_k", "cmp_k_w1", "cmp_k_w2", "cmp_pos_v", "cmp_v_w1", "cmp_v_w2", "nsa_w_o",
          "conv_w", "conv_b", "conv_ln_g", "conv_ln_b", "conv_w_o", "w_out",
          "ffn2_norm", "ffn2_w_gate", "ffn2_w_up", "ffn2_w_down")


def kernel(x, ffn1_norm, ffn1_w_gate, ffn1_w_up, ffn1_w_down, mix_norm, w_in, q_norm, k_norm, cmp_pos_k, cmp_k_w1, cmp_k_w2, cmp_pos_v, cmp_v_w1, cmp_v_w2, nsa_w_o, conv_w, conv_b, conv_ln_g, conv_ln_b, conv_w_o, w_out, ffn2_norm, ffn2_w_gate, ffn2_w_up, ffn2_w_down):
    params = (ffn1_norm, ffn1_w_gate, ffn1_w_up, ffn1_w_down, mix_norm, w_in, q_norm, k_norm, cmp_pos_k,
              cmp_k_w1, cmp_k_w2, cmp_pos_v, cmp_v_w1, cmp_v_w2, nsa_w_o, conv_w, conv_b, conv_ln_g,
              conv_ln_b, conv_w_o, w_out, ffn2_norm, ffn2_w_gate, ffn2_w_up, ffn2_w_down)
    b, t, d = x.shape
    depth = ffn1_norm.shape[0]
    xs = x.reshape(b * t, d)
    outs = []
    for bi in range(b):
        xb = xs[bi * t:(bi + 1) * t]
        for l in range(depth):
            xb = _layer(xb, {n: a[l] for n, a in zip(_NAMES, params)})
        outs.append(xb)
    out = outs[0] if b == 1 else jnp.concatenate(outs, axis=0)
    return out.reshape(b, t, d)
```

```python
import functools
from typing import NamedTuple, Optional

import numpy as np
import jax
import jax.numpy as jnp
from jax import lax
from jax.experimental import pallas as pl
from jax.experimental.pallas import tpu as pltpu

HEAD_DIM = 128
N_HEADS = 16
N_KV_HEADS = 4
GROUP = N_HEADS // N_KV_HEADS
CMP_BLOCK = 32
CMP_STRIDE = 16
SEL_BLOCK = 64
N_SELECT = 16
WINDOW = 512
CONV_WIDTH = 31
NEG_INF = -1e30
FORCE = 1e9
EPS = 1e-6

Q_W = N_HEADS * HEAD_DIM
KV_W = N_KV_HEADS * HEAD_DIM
NSA_GATE_W = N_HEADS * 3

MXU_DTYPE = jnp.bfloat16
LANES = 128
V7X_VMEM_BYTES = 64 * 1024 * 1024
VMEM_CAP = V7X_VMEM_BYTES - 6 * 1024 * 1024

F32 = jnp.float32


def _nbytes(shape, dtype):
    return int(np.prod(shape)) * jnp.dtype(dtype).itemsize


def _params(semantics, blocks, extra=0):
    need = 2 * sum(_nbytes(s, d) for s, d in blocks) + extra + (2 << 20)
    return pltpu.CompilerParams(dimension_semantics=semantics,
                                vmem_limit_bytes=int(min(max(need, 16 << 20), VMEM_CAP)))


def _tile(n, pref):
    return pref if n % pref == 0 else n


def _rmsnorm_kernel(x_ref, g_ref, o_ref):
    x = x_ref[...]
    ms = jnp.mean(x * x, axis=-1, keepdims=True)
    o_ref[...] = (x * lax.rsqrt(ms + EPS) * g_ref[...]).astype(o_ref.dtype)


def rmsnorm_cast(x, gain):
    m, d = x.shape
    tm = _tile(m, 256)
    return pl.pallas_call(
        _rmsnorm_kernel,
        grid=(m // tm,),
        in_specs=[pl.BlockSpec((tm, d), lambda i: (i, 0)),
                  pl.BlockSpec((1, d), lambda i: (0, 0))],
        out_specs=pl.BlockSpec((tm, d), lambda i: (i, 0)),
        out_shape=jax.ShapeDtypeStruct((m, d), MXU_DTYPE),
        compiler_params=_params(("parallel",), [((tm, d), F32), ((tm, d), MXU_DTYPE)],
                                extra=2 * _nbytes((tm, d), F32)),
        name="rmsnorm_cast",
    )(x, gain.reshape(1, d).astype(F32))


class Weight(NamedTuple):
    arr: jax.Array
    row0: Optional[int] = None


def _w_spec(w, k, tn, block_of):
    if w.row0 is None:
        return pl.BlockSpec((k, tn), lambda i, j: (0, block_of(j)))
    assert w.row0 % 8 == 0 and tn % 8 == 0
    return pl.BlockSpec((pl.Element(tn), pl.Element(k)),
                        lambda i, j: (pl.multiple_of(w.row0 + block_of(j) * tn, 8), 0))


def _w_block(w, k, tn):
    return ((k, tn) if w.row0 is None else (tn, k)), w.arr.dtype


def _mxu(x, w, transposed):
    if transposed:
        return lax.dot_general(x, w, (((1,), (1,)), ((), ())), preferred_element_type=F32)
    return jnp.dot(x, w, preferred_element_type=F32)


ROW_CHUNK = 512


def _row_chunked(rows, dots, epilogue):
    rc = min(rows, ROW_CHUNK)
    sl = [slice(c * rc, (c + 1) * rc) for c in range(rows // rc)]
    pending = dots(sl[0])
    for c, s in enumerate(sl):
        cur = pending
        if c + 1 < len(sl):
            pending = dots(sl[c + 1])
        epilogue(s, cur)


def _dual_kernel(*refs, act, transposed, has_cast):
    if has_cast:
        x_ref, w1_ref, w2_ref, c_ref, o_ref, co_ref = refs
        co_ref[...] = c_ref[...].astype(co_ref.dtype)
    else:
        x_ref, w1_ref, w2_ref, o_ref = refs
    w1 = w1_ref[...].astype(x_ref.dtype)
    w2 = w2_ref[...].astype(x_ref.dtype)

    def dots(s):
        x = x_ref[s, :]
        return _mxu(x, w1, transposed), _mxu(x, w2, transposed)

    def epilogue(s, ab):
        o_ref[s, :] = (act(ab[0]) * ab[1]).astype(o_ref.dtype)

    _row_chunked(x_ref.shape[0], dots, epilogue)


def dual_mm(x, w1, w2, n, act, out_dtype, name, tm=1024, tn=512, cast=None, x_buffers=None):
    m, k = x.shape
    tm, tn = _tile(m, tm), _tile(n, tn)
    assert n % tn == 0 and (w1.row0 is None) == (w2.row0 is None)
    block_of = lambda j: j
    ni, nj = m // tm, n // tn
    x_mode = {} if x_buffers is None else {"pipeline_mode": pl.Buffered(x_buffers)}
    in_specs = [pl.BlockSpec((tm, k), lambda i, j: (i, 0), **x_mode),
                _w_spec(w1, k, tn, block_of),
                _w_spec(w2, k, tn, block_of)]
    out_specs = pl.BlockSpec((tm, tn), lambda i, j: (i, j))
    out_shape = jax.ShapeDtypeStruct((m, n), out_dtype)
    blocks = [((tm, k), x.dtype), _w_block(w1, k, tn), _w_block(w2, k, tn), ((tm, tn), out_dtype)]
    args = [x, w1.arr, w2.arr]
    if cast is not None:
        rows, cols = cast.shape
        slab = rows // (ni * nj)
        assert slab * ni * nj == rows and slab % 16 == 0
        slab_spec = pl.BlockSpec((slab, cols), lambda i, j: (i * nj + j, 0))
        in_specs.append(slab_spec)
        out_specs = [out_specs, slab_spec]
        out_shape = [out_shape, jax.ShapeDtypeStruct((rows, cols), MXU_DTYPE)]
        blocks += [((slab, cols), cast.dtype), ((slab, cols), MXU_DTYPE)]
        args.append(cast)
    return pl.pallas_call(
        functools.partial(_dual_kernel, act=act, transposed=w1.row0 is not None, has_cast=cast is not None),
        grid=(ni, nj),
        in_specs=in_specs,
        out_specs=out_specs,
        out_shape=out_shape,
        compiler_params=_params(("parallel", "arbitrary"), blocks, extra=4 * _nbytes((tm, tn), F32)),
        name=name,
    )(*args)


def _mm_kernel(x_ref, w_ref, g_ref, o_ref, *, mode, transposed):
    def epilogue(s, acc):
        if mode == "headnorm":
            for c in range(acc.shape[1] // HEAD_DIM):
                sl = slice(c * HEAD_DIM, (c + 1) * HEAD_DIM)
                v = acc[:, sl]
                ms = jnp.mean(v * v, axis=-1, keepdims=True)
                o_ref[s, sl] = (v * lax.rsqrt(ms + EPS) * g_ref[:, sl]).astype(o_ref.dtype)
        elif mode == "sigmoid":
            o_ref[s, :] = jax.nn.sigmoid(acc).astype(o_ref.dtype)
        else:
            o_ref[s, :] = acc.astype(o_ref.dtype)

    w = w_ref[...].astype(x_ref.dtype)
    _row_chunked(x_ref.shape[0], lambda s: _mxu(x_ref[s, :], w, transposed), epilogue)


def mm(x, w, n, mode, out_dtype, name, gain=None, block_of=None):
    m, k = x.shape
    tm, tn = _tile(m, 1024), _tile(n, 512)
    if gain is None:
        gain = jnp.ones((1, n), F32)
    if block_of is None:
        block_of = lambda j: j
    return pl.pallas_call(
        functools.partial(_mm_kernel, mode=mode, transposed=w.row0 is not None),
        grid=(m // tm, n // tn),
        in_specs=[pl.BlockSpec((tm, k), lambda i, j: (i, 0)),
                  _w_spec(w, k, tn, block_of),
                  pl.BlockSpec((1, tn), lambda i, j: (0, j))],
        out_specs=pl.BlockSpec((tm, tn), lambda i, j: (i, j)),
        out_shape=jax.ShapeDtypeStruct((m, n), out_dtype),
        compiler_params=_params(("parallel", "arbitrary"),
                                [((tm, k), x.dtype), _w_block(w, k, tn), ((tm, tn), out_dtype)],
                                extra=3 * _nbytes((tm, tn), F32)),
        name=name,
    )(x, w.arr, gain.reshape(1, n).astype(F32))


def _mm_acc_kernel(*refs, scale, nk, has_tail):
    if has_tail:
        h_ref, w_ref, ht_ref, wt_ref, r_ref, o_ref, acc_ref = refs
    else:
        h_ref, w_ref, r_ref, o_ref, acc_ref = refs
    if nk == 1 and not has_tail:
        w = w_ref[...].astype(h_ref.dtype)

        def epilogue(s, acc):
            o_ref[s, :] = (r_ref[s, :] + scale * acc).astype(o_ref.dtype)

        _row_chunked(h_ref.shape[0], lambda s: _mxu(h_ref[s, :], w, False), epilogue)
        return
    kk = pl.program_id(2)

    @pl.when(kk == 0)
    def _():
        if has_tail:
            ht = ht_ref[...]
            acc_ref[...] = _mxu(ht, wt_ref[...].astype(ht.dtype), False)
        else:
            acc_ref[...] = jnp.zeros_like(acc_ref)

    h = h_ref[...]
    acc_ref[...] += _mxu(h, w_ref[...].astype(h.dtype), False)

    @pl.when(kk == nk - 1)
    def _():
        o_ref[...] = (r_ref[...] + scale * acc_ref[...]).astype(o_ref.dtype)


def _k_tiling(k, nk):
    if k % (nk * LANES) == 0:
        return k // nk, nk, 0
    tk = k // nk // LANES * LANES
    tail = k - nk * tk
    if tk > 0 and tail % LANES == 0 and (nk * tk) % tail == 0:
        return tk, nk, tail
    return k, 1, 0


def mm_resid(h, w, resid, scale, nk, name, tm=1024, tn=1024, h_buffers=None):
    m, k = h.shape
    n = w.shape[1]
    tm, tn = _tile(m, tm), _tile(n, tn)
    tk, nk, tail = _k_tiling(k, nk)
    h_mode = {} if h_buffers is None else {"pipeline_mode": pl.Buffered(h_buffers)}
    in_specs = [pl.BlockSpec((tm, tk), lambda i, j, kk: (i, kk), **h_mode),
                pl.BlockSpec((tk, tn), lambda i, j, kk: (kk, j))]
    blocks = [((tm, tk), h.dtype), ((tk, tn), w.dtype), ((tm, tn), F32), ((tm, tn), F32)]
    args = [h, w]
    if tail:
        tb = nk * tk // tail
        in_specs += [pl.BlockSpec((tm, tail), lambda i, j, kk: (i, tb)),
                     pl.BlockSpec((tail, tn), lambda i, j, kk: (tb, j))]
        blocks += [((tm, tail), h.dtype), ((tail, tn), w.dtype)]
        args += [h, w]
    in_specs.append(pl.BlockSpec((tm, tn), lambda i, j, kk: (i, j)))
    return pl.pallas_call(
        functools.partial(_mm_acc_kernel, scale=scale, nk=nk, has_tail=bool(tail)),
        grid=(m // tm, n // tn, nk),
        in_specs=in_specs,
        out_specs=pl.BlockSpec((tm, tn), lambda i, j, kk: (i, j)),
        out_shape=jax.ShapeDtypeStruct((m, n), F32),
        scratch_shapes=[pltpu.VMEM((tm, tn), F32)],
        compiler_params=_params(("parallel", "arbitrary", "arbitrary"), blocks,
                                extra=3 * _nbytes((tm, tn), F32)),
        name=name,
    )(*args, resid)


def _gated_kernel(*refs, has_prev, wo_transposed, wm_transposed):
    if has_prev:
        a_ref, wo_ref, h_ref, wm_ref, p_ref, o_ref = refs
    else:
        a_ref, wo_ref, h_ref, wm_ref, o_ref = refs
    a = a_ref[...]
    h = h_ref[...]
    y = _mxu(a, wo_ref[...].astype(a.dtype), wo_transposed)
    g = jax.nn.sigmoid(_mxu(h, wm_ref[...].astype(h.dtype), wm_transposed))
    r = g * y
    if has_prev:
        r = p_ref[...] + r
    o_ref[...] = r.astype(o_ref.dtype)


def gated_mm(a, wo, h, wm, n, prev, out_dtype, name):
    m, ka = a.shape
    kh = h.shape[1]
    tm, tn = _tile(m, 1024), _tile(n, 256)
    has_prev = prev is not None
    ident = lambda j: j
    in_specs = [pl.BlockSpec((tm, ka), lambda i, j: (i, 0)),
                _w_spec(wo, ka, tn, ident),
                pl.BlockSpec((tm, kh), lambda i, j: (i, 0)),
                _w_spec(wm, kh, tn, ident)]
    blocks = [((tm, ka), a.dtype), _w_block(wo, ka, tn), ((tm, kh), h.dtype), _w_block(wm, kh, tn),
              ((tm, tn), out_dtype)]
    args = [a, wo.arr, h, wm.arr]
    if has_prev:
        in_specs.append(pl.BlockSpec((tm, tn), lambda i, j: (i, j)))
        blocks.append(((tm, tn), prev.dtype))
        args.append(prev)
    return pl.pallas_call(
        functools.partial(_gated_kernel, has_prev=has_prev, wo_transposed=wo.row0 is not None,
                          wm_transposed=wm.row0 is not None),
        grid=(m // tm, n // tn),
        in_specs=in_specs,
        out_specs=pl.BlockSpec((tm, tn), lambda i, j: (i, j)),
        out_shape=jax.ShapeDtypeStruct((m, n), out_dtype),
        compiler_params=_params(("parallel", "arbitrary"), blocks, extra=4 * _nbytes((tm, tn), F32)),
        name=name,
    )(*args)


def _gelu_tanh(x):
    return 0.5 * x * (1.0 + jnp.tanh(np.sqrt(2.0 / np.pi).astype(np.float32) * (x + 0.044715 * (x * x * x))))


def _compress_kernel(a_ref, pa_ref, pb_ref, w1a_ref, w1b_ref, w2_ref, g_ref, o_ref, *, norm):
    nch = a_ref.shape[0] // CMP_STRIDE
    a = jnp.concatenate([a_ref[pl.ds(tok, nch, stride=CMP_STRIDE), :] for tok in range(CMP_STRIDE)], axis=1)
    ua = jnp.dot((a + pa_ref[...]).astype(MXU_DTYPE), w1a_ref[...], preferred_element_type=F32)
    ub = jnp.dot((a + pb_ref[...]).astype(MXU_DTYPE), w1b_ref[...], preferred_element_type=F32)
    pre = ua + pltpu.roll(ub, nch - 1, 0)
    hid = _gelu_tanh(pre).astype(MXU_DTYPE)
    out = jnp.dot(hid, w2_ref[...], preferred_element_type=F32)
    if norm:
        ms = jnp.mean(out * out, axis=-1, keepdims=True)
        out = out * lax.rsqrt(ms + EPS) * g_ref[...]
    o_ref[0] = out.astype(o_ref.dtype)


def compress(kv, head0, pos, w1, w2, gain, norm, name):
    t = kv.shape[0]
    hkv, nch, cw = N_KV_HEADS, t // CMP_STRIDE, CMP_STRIDE * HEAD_DIM
    hid = w1.shape[1]
    posf = pos.reshape(1, CMP_BLOCK * HEAD_DIM).astype(F32)
    pa, pb = posf[:, :cw], posf[:, cw:]
    w1c = w1.astype(MXU_DTYPE)
    w1a, w1b = w1c[:cw], w1c[cw:]
    full = lambda shape: pl.BlockSpec(shape, lambda h: (0,) * len(shape))
    return pl.pallas_call(
        functools.partial(_compress_kernel, norm=norm),
        grid=(hkv,),
        in_specs=[pl.BlockSpec((t, HEAD_DIM), lambda h: (0, head0 + h)),
                  full((1, cw)), full((1, cw)), full((cw, hid)), full((cw, hid)),
                  full((hid, HEAD_DIM)), full((1, HEAD_DIM))],
        out_specs=pl.BlockSpec((1, nch, HEAD_DIM), lambda h: (h, 0, 0)),
        out_shape=jax.ShapeDtypeStruct((hkv, nch, HEAD_DIM), MXU_DTYPE),
        compiler_params=_params(("parallel",),
                                [((nch, cw), F32), ((cw, hid), MXU_DTYPE), ((cw, hid), MXU_DTYPE)],
                                extra=8 * _nbytes((nch, cw), F32)),
        name=name,
    )(kv, pa, pb, w1a, w1b, w2.astype(MXU_DTYPE), gain.reshape(1, HEAD_DIM).astype(F32))


def _conv_kernel(cur_ref, prev_ref, cw_ref, cb_ref, lg_ref, lb_ref, o_ref, ext_ref, xs_ref, y_ref, *,
                 halo, rc, cc):
    i = pl.program_id(0)
    tm, ch = cur_ref.shape
    sub = 8
    ext_ref[0:halo, :] = jnp.where(i > 0, prev_ref[tm - halo:tm, :], 0.0)
    ext_ref[halo:halo + tm, :] = cur_ref[...]
    base = halo - (CONV_WIDTH - 1)
    span = xs_ref.shape[1]
    for c0 in range(0, ch, cc):
        for r in range(1, sub):
            xs_ref[r - 1] = ext_ref[r:r + span, c0:c0 + cc]
        for r0 in range(0, tm, rc):
            acc = jnp.zeros((rc, cc), F32) + cb_ref[:, c0:c0 + cc]
            for w in range(CONV_WIDTH):
                a, r = divmod(base + r0 + w, sub)
                if r == 0:
                    tap = ext_ref[a * sub:a * sub + rc, c0:c0 + cc]
                else:
                    tap = xs_ref[r - 1, a * sub:a * sub + rc, :]
                acc = acc + tap * cw_ref[w:w + 1, c0:c0 + cc]
            y_ref[r0:r0 + rc, c0:c0 + cc] = acc
    y = y_ref[...]
    mu = jnp.mean(y, axis=-1, keepdims=True)
    d = y - mu
    var = jnp.mean(d * d, axis=-1, keepdims=True)
    z = d * lax.rsqrt(var + EPS) * lg_ref[...] + lb_ref[...]
    o_ref[...] = (z * jax.nn.sigmoid(z)).astype(o_ref.dtype)


def conv_module(u, conv_w, conv_b, ln_g, ln_b):
    t, ch = u.shape
    tm = _tile(t, 256)
    halo = 32
    row = lambda v: v.reshape(1, ch).astype(F32)
    full = lambda shape: pl.BlockSpec(shape, lambda i: (0, 0))
    cc = min(512, ch)
    return pl.pallas_call(
        functools.partial(_conv_kernel, halo=halo, rc=64, cc=cc),
        grid=(t // tm,),
        in_specs=[pl.BlockSpec((tm, ch), lambda i: (i, 0)),
                  pl.BlockSpec((tm, ch), lambda i: (jnp.maximum(i - 1, 0), 0)),
                  full((CONV_WIDTH, ch)), full((1, ch)), full((1, ch)), full((1, ch))],
        out_specs=pl.BlockSpec((tm, ch), lambda i: (i, 0)),
        out_shape=jax.ShapeDtypeStruct((t, ch), MXU_DTYPE),
        scratch_shapes=[pltpu.VMEM((tm + halo, ch), F32), pltpu.VMEM((7, tm + halo - 8, cc), F32),
                        pltpu.VMEM((tm, ch), F32)],
        compiler_params=_params(("parallel",),
                                [((tm, ch), F32), ((tm, ch), F32), ((tm, ch), MXU_DTYPE)],
                                extra=6 * _nbytes((tm, ch), F32)),
        name="conv_module",
    )(u, u, conv_w.astype(F32), row(conv_b), row(ln_g), row(ln_b))


def _dot_t(a, b):
    return lax.dot_general(a, b, (((0,), (0,)), ((), ())), preferred_element_type=F32)


def _sort_desc(c):
    n = len(c)
    if n == 1:
        return list(c)
    return _merge_desc(_sort_desc(c[:n // 2]) + _sort_desc(c[n // 2:])[::-1])


def _merge_desc(c):
    c = list(c)
    n = len(c)
    d = n // 2
    while d >= 1:
        for a in range(n):
            if a & d == 0:
                c[a], c[a + d] = jnp.maximum(c[a], c[a + d]), jnp.minimum(c[a], c[a + d])
        d //= 2
    return c


def _kth_largest(score, k):
    ns = score.shape[0]
    sub = 8
    a = _sort_desc([score[j * sub:(j + 1) * sub, :] for j in range(ns // sub)])[:k]
    shift = sub // 2
    while True:
        b = [pltpu.roll(v, shift, 0) for v in a]
        if 2 * len(a) <= k:
            top = a + b[::-1]
        else:
            top = [jnp.maximum(a[j], b[len(a) - 1 - j]) for j in range(len(a))]
        if shift == 1:
            break
        a = _merge_desc(top)
        shift //= 2
    kth = top[0]
    for v in top[1:]:
        kth = jnp.minimum(kth, v)
    return kth[0:1, :]


def _nsa_kernel(q_ref, kc_ref, vc_ref, ks_ref, kw_ref, vs_ref, vw_ref, gt_ref, mt_ref, lt_ref, e_ref, o_ref,
                qx_ref, biasx_ref, m_ref, l_ref, acc_ref, out_ref, sa_ref, sb_ref, cs_ref, ws_ref, *,
                tq, n_top, wl):
    i = pl.program_id(1)
    t0 = i * tq
    nc = kc_ref.shape[1]
    ns = mt_ref.shape[0]
    nb = tq // SEL_BLOCK
    gsl = [slice(g * tq, (g + 1) * tq) for g in range(GROUP)]

    def gate(g, b):
        return gt_ref[0, g * 3 + b:g * 3 + b + 1, :]

    for g in range(GROUP):
        qg = q_ref[:, g * HEAD_DIM:(g + 1) * HEAD_DIM].astype(F32)
        qx_ref[0:HEAD_DIM, gsl[g]] = qg.T.astype(MXU_DTYPE)
    qx_ref[HEAD_DIM:, :] = jnp.zeros((HEAD_DIM, GROUP * tq), MXU_DTYPE)
    t_q = t0 + lax.broadcasted_iota(jnp.int32, (1, tq), 1)

    kc = kc_ref[0]
    vc = vc_ref[0]
    ws = pl.multiple_of(jnp.maximum(t0 + tq - wl, 0), SEL_BLOCK)
    kw = kw_ref[pl.ds(ws, wl), :]
    vw = vw_ref[pl.ds(ws, wl), :]
    for g in range(GROUP):
        cs_ref[g] = jnp.dot(kc, qx_ref[0:HEAD_DIM, gsl[g]], preferred_element_type=F32)
    for g in range(GROUP):
        ws_ref[g] = jnp.dot(kw, qx_ref[0:HEAD_DIM, gsl[g]], preferred_element_type=F32)

    cmp_end = lax.broadcasted_iota(jnp.int32, (nc, 1), 0) * CMP_STRIDE + (CMP_BLOCK - 1)
    cbias = jnp.where(cmp_end <= t_q, 0.0, NEG_INF)
    psum = None
    for g in range(GROUP):
        sm = cs_ref[g] + cbias
        mx = jnp.maximum(jnp.max(sm, axis=0, keepdims=True), 0.5 * NEG_INF)
        e = jnp.exp2(sm - mx)
        den = jnp.sum(e, axis=0, keepdims=True)
        p = e * (1.0 / jnp.where(den > 0.0, den, 1.0))
        out_ref[:, gsl[g]] = gate(g, 0) * _dot_t(vc, p.astype(MXU_DTYPE))
        psum = p if psum is None else psum + p

    p_hi = psum.astype(MXU_DTYPE)
    p_lo = (psum - p_hi.astype(F32)).astype(MXU_DTYPE)
    imp = (jnp.dot(mt_ref[...], p_hi, preferred_element_type=F32)
           + jnp.dot(mt_ref[...], p_lo, preferred_element_type=F32))
    s_id = lax.broadcasted_iota(jnp.int32, (ns, 1), 0)
    cur = t_q // SEL_BLOCK
    forced = (s_id == 0) | (s_id == cur) | (s_id == cur - 1)
    valid = s_id * SEL_BLOCK <= t_q
    score = jnp.where(forced, FORCE, jnp.where(valid, imp, NEG_INF))
    kth = _kth_largest(score, n_top)
    above = score > kth
    equal = score == kth
    n_above = jnp.sum(jnp.where(above, 1.0, 0.0), axis=0, keepdims=True)
    n_equal_upto = jnp.dot(lt_ref[...], jnp.where(equal, 1.0, 0.0).astype(MXU_DTYPE),
                           preferred_element_type=F32)
    picked = above | (equal & (n_equal_upto <= n_top - n_above))
    block_bias = jnp.where(picked, 0.0, NEG_INF)
    pad_rows = jnp.zeros((biasx_ref.shape[1] - nb, tq), F32)
    for j in range(ns // nb):
        slab = jnp.concatenate([block_bias[j * nb:(j + 1) * nb, :], pad_rows], axis=0)
        biasx_ref[j] = jnp.concatenate([slab] * GROUP, axis=1).astype(MXU_DTYPE)

    m_ref[...] = jnp.full(m_ref.shape, NEG_INF, F32)
    l_ref[...] = jnp.zeros(l_ref.shape, F32)
    acc_ref[...] = jnp.zeros(acc_ref.shape, F32)

    def issue_scores(kt, s_ref):
        k0 = pl.multiple_of(kt * tq, tq)
        kx = jnp.concatenate([ks_ref[pl.ds(k0, tq), :], e_ref[...]], axis=1)
        qx_ref[HEAD_DIM:HEAD_DIM + biasx_ref.shape[1], :] = biasx_ref[kt]
        for g in range(GROUP):
            s_ref[g] = jnp.dot(kx, qx_ref[:, gsl[g]], preferred_element_type=F32)

    def consume(kt, s_ref, diagonal):
        k0 = pl.multiple_of(kt * tq, tq)
        vv = vs_ref[pl.ds(k0, tq), :]
        if diagonal:
            causal = k0 + lax.broadcasted_iota(jnp.int32, (tq, 1), 0) <= t_q
        for g in range(GROUP):
            sm = jnp.where(causal, s_ref[g], NEG_INF) if diagonal else s_ref[g]
            m_old = m_ref[:, gsl[g]]
            m_new = jnp.maximum(m_old, jnp.max(sm, axis=0, keepdims=True))
            alpha = jnp.exp2(m_old - m_new)
            pt = jnp.exp2(sm - m_new)
            l_ref[:, gsl[g]] = alpha * l_ref[:, gsl[g]] + jnp.sum(pt, axis=0, keepdims=True)
            acc_ref[:, gsl[g]] = alpha * acc_ref[:, gsl[g]] + _dot_t(vv, pt.astype(MXU_DTYPE))
            m_ref[:, gsl[g]] = m_new

    def stage(kt, s_cur, s_next):
        issue_scores(kt + 1, s_next)
        consume(kt, s_cur, False)

    issue_scores(0, sa_ref)

    back = t_q - (ws + lax.broadcasted_iota(jnp.int32, (wl, 1), 0))
    wbias = jnp.where(lax.bitcast_convert_type(back, jnp.uint32) < WINDOW, 0.0, NEG_INF)
    for g in range(GROUP):
        sm = ws_ref[g] + wbias
        e = jnp.exp2(sm - jnp.max(sm, axis=0, keepdims=True))
        inv = 1.0 / jnp.sum(e, axis=0, keepdims=True)
        out_ref[:, gsl[g]] += (gate(g, 2) * inv) * _dot_t(vw, e.astype(MXU_DTYPE))

    def pair(j, carry):
        stage(2 * j, sa_ref, sb_ref)
        stage(2 * j + 1, sb_ref, sa_ref)
        return carry

    lax.fori_loop(0, i // 2, pair, 0)

    @pl.when(i % 2 == 0)
    def _():
        consume(i, sa_ref, True)

    @pl.when(i % 2 == 1)
    def _():
        stage(i - 1, sa_ref, sb_ref)
        consume(i, sb_ref, True)

    for g in range(GROUP):
        out_t = out_ref[:, gsl[g]] + (gate(g, 1) / l_ref[:, gsl[g]]) * acc_ref[:, gsl[g]]
        o_ref[:, g * HEAD_DIM:(g + 1) * HEAD_DIM] = out_t.T.astype(o_ref.dtype)


def nsa_attention(q, k_cmp, v_cmp, k_sw, v_sw, gates_t, tq=256):
    t = q.shape[0]
    tq = min(tq, t)
    wl = min(WINDOW + tq, t)
    nc = k_cmp.shape[1]
    ns = t // SEL_BLOCK
    n_top = min(N_SELECT, ns)
    r = GROUP * tq
    lt = jnp.asarray(np.tril(np.ones((ns, ns))), MXU_DTYPE)
    cj = np.arange(nc)[None, :] * CMP_STRIDE
    si = np.arange(ns)[:, None] * SEL_BLOCK
    ov = np.clip(np.minimum(cj + CMP_BLOCK, si + SEL_BLOCK) - np.maximum(cj, si), 0, None) / CMP_BLOCK
    ov[:, (t - CMP_BLOCK) // CMP_STRIDE + 1:] = 0.0
    mt = jnp.asarray(ov, MXU_DTYPE)
    nb = tq // SEL_BLOCK
    onehot = jnp.asarray(np.arange(tq)[:, None] // SEL_BLOCK == np.arange(HEAD_DIM)[None, :], MXU_DTYPE)
    bias_rows = 16
    assert nb <= bias_rows and ns % nb == 0

    head = lambda off: pl.BlockSpec((t, HEAD_DIM), lambda h, i: (0, off + h))
    return pl.pallas_call(
        functools.partial(_nsa_kernel, tq=tq, n_top=n_top, wl=wl),
        grid=(N_KV_HEADS, t // tq),
        in_specs=[pl.BlockSpec((tq, GROUP * HEAD_DIM), lambda h, i: (i, h)),
                  pl.BlockSpec((1, nc, HEAD_DIM), lambda h, i: (h, 0, 0)),
                  pl.BlockSpec((1, nc, HEAD_DIM), lambda h, i: (h, 0, 0)),
                  head(0), head(N_KV_HEADS), head(0), head(N_KV_HEADS),
                  pl.BlockSpec((1, 16, tq), lambda h, i: (h, 0, i)),
                  pl.BlockSpec((ns, nc), lambda h, i: (0, 0)),
                  pl.BlockSpec((ns, ns), lambda h, i: (0, 0)),
                  pl.BlockSpec((tq, HEAD_DIM), lambda h, i: (0, 0))],
        out_specs=pl.BlockSpec((tq, GROUP * HEAD_DIM), lambda h, i: (i, h)),
        out_shape=jax.ShapeDtypeStruct((t, Q_W), MXU_DTYPE),
        scratch_shapes=[pltpu.VMEM((2 * HEAD_DIM, r), MXU_DTYPE),
                        pltpu.VMEM((ns // nb, bias_rows, r), MXU_DTYPE),
                        pltpu.VMEM((1, r), F32), pltpu.VMEM((1, r), F32),
                        pltpu.VMEM((HEAD_DIM, r), F32), pltpu.VMEM((HEAD_DIM, r), F32),
                        pltpu.VMEM((GROUP, tq, tq), F32), pltpu.VMEM((GROUP, tq, tq), F32),
                        pltpu.VMEM((GROUP, nc, tq), F32), pltpu.VMEM((GROUP, wl, tq), F32)],
        compiler_params=_params(("parallel", "arbitrary"),
                                [((t, HEAD_DIM), MXU_DTYPE)] * 4 + [((nc, HEAD_DIM), MXU_DTYPE)] * 2
                                + [((tq, GROUP * HEAD_DIM), MXU_DTYPE)] * 2 + [((ns, nc), MXU_DTYPE)],
                                extra=12 * _nbytes((max(nc, wl), tq), F32)),
        name="nsa_attention",
    )(q, k_cmp, v_cmp, k_sw, k_sw, v_sw, v_sw, gates_t, mt, lt, onehot)


def _ffn(x, norm_g, w_gate, w_up, w_down):
    f = w_gate.shape[1]
    xn = rmsnorm_cast(x, norm_g)
    h, wd = dual_mm(xn, Weight(w_gate), Weight(w_up), f, jax.nn.silu, MXU_DTYPE, "ffn_gate_up", tm=2048, tn=256,
                    cast=w_down, x_buffers=1)
    return mm_resid(h, wd, x, 0.5, 4, "ffn_down")


def _layer(x, p):
    t, d = x.shape
    x = _ffn(x, p["ffn1_norm"], p["ffn1_w_gate"], p["ffn1_w_up"], p["ffn1_w_down"])

    hn = rmsnorm_cast(x, p["mix_norm"])
    w_in = p["w_in"]
    conv_ch = p["conv_w"].shape[1]
    assert Q_W % KV_W == 0
    qb = Q_W // KV_W
    w_t = w_in.T
    qkv = Weight(w_t, 0)
    o = Q_W + 6 * KV_W
    assert o % 8 == 0 and NSA_GATE_W % 8 == 0 and conv_ch % 8 == 0 and w_t.shape[0] >= o + LANES
    w_gn = Weight(w_t, o); o += NSA_GATE_W
    w_ga = Weight(w_t, o); o += conv_ch
    w_gg = Weight(w_t, o); o += conv_ch
    w_m0 = Weight(w_t, o); o += d
    w_m1 = Weight(w_t, o); o += d

    scale = HEAD_DIM ** -0.5 * np.log2(np.e)
    q = mm(hn, qkv, Q_W, "headnorm", MXU_DTYPE, "proj_q", gain=jnp.tile(p["q_norm"] * scale, N_HEADS))
    k_sw = mm(hn, qkv, 2 * KV_W, "headnorm", MXU_DTYPE, "proj_k", block_of=lambda j: qb + 2 + 2 * j,
              gain=jnp.concatenate([jnp.tile(p["k_norm"][1], N_KV_HEADS), jnp.tile(p["k_norm"][2], N_KV_HEADS)]))
    v_sw = mm(hn, qkv, 2 * KV_W, "plain", MXU_DTYPE, "proj_v", block_of=lambda j: qb + 3 + 2 * j)
    kv_c = mm(hn, qkv, 2 * KV_W, "plain", F32, "proj_kv_cmp", block_of=lambda j: qb + j)
    g_nsa = mm(hn, w_gn, LANES, "sigmoid", F32, "proj_gates")
    u = dual_mm(hn, w_gg, w_ga, conv_ch, jax.nn.sigmoid, F32, "proj_glu", tn=256)

    k_cmp = compress(kv_c, 0, p["cmp_pos_k"], p["cmp_k_w1"], p["cmp_k_w2"], p["k_norm"][0], True, "compress_k")
    v_cmp = compress(kv_c, N_KV_HEADS, p["cmp_pos_v"], p["cmp_v_w1"], p["cmp_v_w2"], p["k_norm"][0], False,
                     "compress_v")

    gates_t = g_nsa[:, :NSA_GATE_W].reshape(t, N_KV_HEADS, GROUP * 3).transpose(1, 2, 0)
    gates_t = jnp.pad(gates_t, ((0, 0), (0, 16 - GROUP * 3), (0, 0)))
    attn = nsa_attention(q, k_cmp, v_cmp, k_sw, v_sw, gates_t)

    uc = conv_module(u, p["conv_w"], p["conv_b"], p["conv_ln_g"], p["conv_ln_b"])

    mix = gated_mm(attn, Weight(p["nsa_w_o"]), hn, w_m0, d, None, F32, "merge_nsa")
    mix = gated_mm(uc, Weight(p["conv_w_o"]), hn, w_m1, d, mix, MXU_DTYPE, "merge_conv")
    x = mm_resid(mix, p["w_out"], x, 1.0, 1, "mix_out", tm=2048, tn=256, h_buffers=1)

    return _ffn(x, p["ffn2_norm"], p["ffn2_w_gate"], p["ffn2_w_up"], p["ffn2_w_down"])


_NAMES = ("ffn1_norm", "ffn1_w_gate", "ffn1_w_up", "ffn1_w_down", "mix_norm", "w_in", "q_norm", "k_norm",
          "cmp_pos_k", "cmp_k_w1", "cmp_k_w2", "cmp_pos_v", "cmp_v_w1", "cmp_v_w2", "nsa_w_o",
          "conv_w", "conv_b", "conv_ln_g", "conv_ln_b", "conv_w_o", "w_out",
          "ffn2_norm", "ffn2_w_gate", "ffn2_w_up", "ffn2_w_down")


def kernel(x, ffn1_norm, ffn1_w_gate, ffn1_w_up, ffn1_w_down, mix_norm, w_in, q_norm, k_norm, cmp_pos_k, cmp_k_w1, cmp_k_w2, cmp_pos_v, cmp_v_w1, cmp_v_w2, nsa_w_o, conv_w, conv_b, conv_ln_g, conv_ln_b, conv_w_o, w_out, ffn2_norm, ffn2_w_gate, ffn2_w_up, ffn2_w_down):
    params = (ffn1_norm, ffn1_w_gate, ffn1_w_up, ffn1_w_down, mix_norm, w_in, q_norm, k_norm, cmp_pos_k,
              cmp_k_w1, cmp_k_w2, cmp_pos_v, cmp_v_w1, cmp_v_w2, nsa_w_o, conv_w, conv_b, conv_ln_g,
              conv_ln_b, conv_w_o, w_out, ffn2_norm, ffn2_w_gate, ffn2_w_up, ffn2_w_down)
    b, t, d = x.shape
    depth = ffn1_norm.shape[0]
    xs = x.reshape(b * t, d)
    outs = []
    for bi in range(b):
        xb = xs[bi * t:(bi + 1) * t]
        for l in range(depth):
            xb = _layer(xb, {n: a[l] for n, a in zip(_NAMES, params)})
        outs.append(xb)
    out = outs[0] if b == 1 else jnp.concatenate(outs, axis=0)
    return out.reshape(b, t, d)
```

```python
import functools
from typing import NamedTuple, Optional

import numpy as np
import jax
import jax.numpy as jnp
from jax import lax
from jax.experimental import pallas as pl
from jax.experimental.pallas import tpu as pltpu

HEAD_DIM = 128
N_HEADS = 16
N_KV_HEADS = 4
GROUP = N_HEADS // N_KV_HEADS
CMP_BLOCK = 32
CMP_STRIDE = 16
SEL_BLOCK = 64
N_SELECT = 16
WINDOW = 512
CONV_WIDTH = 31
NEG_INF = -1e30
FORCE = 1e9
EPS = 1e-6

Q_W = N_HEADS * HEAD_DIM
KV_W = N_KV_HEADS * HEAD_DIM
NSA_GATE_W = N_HEADS * 3

MXU_DTYPE = jnp.bfloat16
LANES = 128
V7X_VMEM_BYTES = 64 * 1024 * 1024
VMEM_CAP = V7X_VMEM_BYTES - 6 * 1024 * 1024

F32 = jnp.float32


def _nbytes(shape, dtype):
    return int(np.prod(shape)) * jnp.dtype(dtype).itemsize


def _params(semantics, blocks, extra=0):
    need = 2 * sum(_nbytes(s, d) for s, d in blocks) + extra + (2 << 20)
    return pltpu.CompilerParams(dimension_semantics=semantics,
                                vmem_limit_bytes=int(min(max(need, 16 << 20), VMEM_CAP)))


def _tile(n, pref):
    return pref if n % pref == 0 else n


def _rmsnorm_kernel(x_ref, g_ref, o_ref):
    x = x_ref[...]
    ms = jnp.mean(x * x, axis=-1, keepdims=True)
    o_ref[...] = (x * lax.rsqrt(ms + EPS) * g_ref[...]).astype(o_ref.dtype)


def rmsnorm_cast(x, gain):
    m, d = x.shape
    tm = _tile(m, 256)
    return pl.pallas_call(
        _rmsnorm_kernel,
        grid=(m // tm,),
        in_specs=[pl.BlockSpec((tm, d), lambda i: (i, 0)),
                  pl.BlockSpec((1, d), lambda i: (0, 0))],
        out_specs=pl.BlockSpec((tm, d), lambda i: (i, 0)),
        out_shape=jax.ShapeDtypeStruct((m, d), MXU_DTYPE),
        compiler_params=_params(("parallel",), [((tm, d), F32), ((tm, d), MXU_DTYPE)],
                                extra=2 * _nbytes((tm, d), F32)),
        name="rmsnorm_cast",
    )(x, gain.reshape(1, d).astype(F32))


class Weight(NamedTuple):
    arr: jax.Array
    row0: Optional[int] = None


def _w_spec(w, k, tn, block_of):
    if w.row0 is None:
        return pl.BlockSpec((k, tn), lambda i, j: (0, block_of(j)))
    assert w.row0 % 8 == 0 and tn % 8 == 0
    return pl.BlockSpec((pl.Element(tn), pl.Element(k)),
                        lambda i, j: (pl.multiple_of(w.row0 + block_of(j) * tn, 8), 0))


def _w_block(w, k, tn):
    return ((k, tn) if w.row0 is None else (tn, k)), w.arr.dtype


def _mxu(x, w, transposed):
    if transposed:
        return lax.dot_general(x, w, (((1,), (1,)), ((), ())), preferred_element_type=F32)
    return jnp.dot(x, w, preferred_element_type=F32)


ROW_CHUNK = 512


def _row_chunked(rows, dots, epilogue):
    rc = min(rows, ROW_CHUNK)
    sl = [slice(c * rc, (c + 1) * rc) for c in range(rows // rc)]
    pending = dots(sl[0])
    for c, s in enumerate(sl):
        cur = pending
        if c + 1 < len(sl):
            pending = dots(sl[c + 1])
        epilogue(s, cur)


def _dual_kernel(*refs, act, transposed, has_cast):
    if has_cast:
        x_ref, w1_ref, w2_ref, c_ref, o_ref, co_ref = refs
        co_ref[...] = c_ref[...].astype(co_ref.dtype)
    else:
        x_ref, w1_ref, w2_ref, o_ref = refs
    w1 = w1_ref[...].astype(x_ref.dtype)
    w2 = w2_ref[...].astype(x_ref.dtype)

    def dots(s):
        x = x_ref[s, :]
        return _mxu(x, w1, transposed), _mxu(x, w2, transposed)

    def epilogue(s, ab):
        o_ref[s, :] = (act(ab[0]) * ab[1]).astype(o_ref.dtype)

    _row_chunked(x_ref.shape[0], dots, epilogue)


def dual_mm(x, w1, w2, n, act, out_dtype, name, tm=1024, tn=512, cast=None, x_buffers=None):
    m, k = x.shape
    tm, tn = _tile(m, tm), _tile(n, tn)
    assert n % tn == 0 and (w1.row0 is None) == (w2.row0 is None)
    block_of = lambda j: j
    ni, nj = m // tm, n // tn
    x_mode = {} if x_buffers is None else {"pipeline_mode": pl.Buffered(x_buffers)}
    in_specs = [pl.BlockSpec((tm, k), lambda i, j: (i, 0), **x_mode),
                _w_spec(w1, k, tn, block_of),
                _w_spec(w2, k, tn, block_of)]
    out_specs = pl.BlockSpec((tm, tn), lambda i, j: (i, j))
    out_shape = jax.ShapeDtypeStruct((m, n), out_dtype)
    blocks = [((tm, k), x.dtype), _w_block(w1, k, tn), _w_block(w2, k, tn), ((tm, tn), out_dtype)]
    args = [x, w1.arr, w2.arr]
    if cast is not None:
        rows, cols = cast.shape
        slab = rows // (ni * nj)
        assert slab * ni * nj == rows and slab % 16 == 0
        slab_spec = pl.BlockSpec((slab, cols), lambda i, j: (i * nj + j, 0))
        in_specs.append(slab_spec)
        out_specs = [out_specs, slab_spec]
        out_shape = [out_shape, jax.ShapeDtypeStruct((rows, cols), MXU_DTYPE)]
        blocks += [((slab, cols), cast.dtype), ((slab, cols), MXU_DTYPE)]
        args.append(cast)
    return pl.pallas_call(
        functools.partial(_dual_kernel, act=act, transposed=w1.row0 is not None, has_cast=cast is not None),
        grid=(ni, nj),
        in_specs=in_specs,
        out_specs=out_specs,
        out_shape=out_shape,
        compiler_params=_params(("parallel", "arbitrary"), blocks, extra=4 * _nbytes((tm, tn), F32)),
        name=name,
    )(*args)


def _mm_kernel(x_ref, w_ref, g_ref, o_ref, *, mode, transposed):
    def epilogue(s, acc):
        if mode == "headnorm":
            for c in range(acc.shape[1] // HEAD_DIM):
                sl = slice(c * HEAD_DIM, (c + 1) * HEAD_DIM)
                v = acc[:, sl]
                ms = jnp.mean(v * v, axis=-1, keepdims=True)
                o_ref[s, sl] = (v * lax.rsqrt(ms + EPS) * g_ref[:, sl]).astype(o_ref.dtype)
        elif mode == "sigmoid":
            o_ref[s, :] = jax.nn.sigmoid(acc).astype(o_ref.dtype)
        else:
            o_ref[s, :] = acc.astype(o_ref.dtype)

    w = w_ref[...].astype(x_ref.dtype)
    _row_chunked(x_ref.shape[0], lambda s: _mxu(x_ref[s, :], w, transposed), epilogue)


def mm(x, w, n, mode, out_dtype, name, gain=None, block_of=None):
    m, k = x.shape
    tm, tn = _tile(m, 2048), _tile(n, 512)
    if gain is None:
        gain = jnp.ones((1, n), F32)
    if block_of is None:
        block_of = lambda j: j
    return pl.pallas_call(
        functools.partial(_mm_kernel, mode=mode, transposed=w.row0 is not None),
        grid=(m // tm, n // tn),
        in_specs=[pl.BlockSpec((tm, k), lambda i, j: (i, 0), pipeline_mode=pl.Buffered(1)),
                  _w_spec(w, k, tn, block_of),
                  pl.BlockSpec((1, tn), lambda i, j: (0, j))],
        out_specs=pl.BlockSpec((tm, tn), lambda i, j: (i, j)),
        out_shape=jax.ShapeDtypeStruct((m, n), out_dtype),
        compiler_params=_params(("parallel", "arbitrary"),
                                [((tm, k), x.dtype), _w_block(w, k, tn), ((tm, tn), out_dtype)],
                                extra=3 * _nbytes((tm, tn), F32)),
        name=name,
    )(x, w.arr, gain.reshape(1, n).astype(F32))


def _mm_acc_kernel(*refs, scale, nk, has_tail):
    if has_tail:
        h_ref, w_ref, ht_ref, wt_ref, r_ref, o_ref, acc_ref = refs
    else:
        h_ref, w_ref, r_ref, o_ref, acc_ref = refs
    if nk == 1 and not has_tail:
        w = w_ref[...].astype(h_ref.dtype)

        def epilogue(s, acc):
            o_ref[s, :] = (r_ref[s, :] + scale * acc).astype(o_ref.dtype)

        _row_chunked(h_ref.shape[0], lambda s: _mxu(h_ref[s, :], w, False), epilogue)
        return
    kk = pl.program_id(2)

    @pl.when(kk == 0)
    def _():
        if has_tail:
            ht = ht_ref[...]
            acc_ref[...] = _mxu(ht, wt_ref[...].astype(ht.dtype), False)
        else:
            acc_ref[...] = jnp.zeros_like(acc_ref)

    h = h_ref[...]
    acc_ref[...] += _mxu(h, w_ref[...].astype(h.dtype), False)

    @pl.when(kk == nk - 1)
    def _():
        o_ref[...] = (r_ref[...] + scale * acc_ref[...]).astype(o_ref.dtype)


def _k_tiling(k, nk):
    if k % (nk * LANES) == 0:
        return k // nk, nk, 0
    tk = k // nk // LANES * LANES
    tail = k - nk * tk
    if tk > 0 and tail % LANES == 0 and (nk * tk) % tail == 0:
        return tk, nk, tail
    return k, 1, 0


def mm_resid(h, w, resid, scale, nk, name, tm=1024, tn=1024, h_buffers=None):
    m, k = h.shape
    n = w.shape[1]
    tm, tn = _tile(m, tm), _tile(n, tn)
    tk, nk, tail = _k_tiling(k, nk)
    h_mode = {} if h_buffers is None else {"pipeline_mode": pl.Buffered(h_buffers)}
    in_specs = [pl.BlockSpec((tm, tk), lambda i, j, kk: (i, kk), **h_mode),
                pl.BlockSpec((tk, tn), lambda i, j, kk: (kk, j))]
    blocks = [((tm, tk), h.dtype), ((tk, tn), w.dtype), ((tm, tn), F32), ((tm, tn), F32)]
    args = [h, w]
    if tail:
        tb = nk * tk // tail
        in_specs += [pl.BlockSpec((tm, tail), lambda i, j, kk: (i, tb)),
                     pl.BlockSpec((tail, tn), lambda i, j, kk: (tb, j))]
        blocks += [((tm, tail), h.dtype), ((tail, tn), w.dtype)]
        args += [h, w]
    in_specs.append(pl.BlockSpec((tm, tn), lambda i, j, kk: (i, j)))
    return pl.pallas_call(
        functools.partial(_mm_acc_kernel, scale=scale, nk=nk, has_tail=bool(tail)),
        grid=(m // tm, n // tn, nk),
        in_specs=in_specs,
        out_specs=pl.BlockSpec((tm, tn), lambda i, j, kk: (i, j)),
        out_shape=jax.ShapeDtypeStruct((m, n), F32),
        scratch_shapes=[pltpu.VMEM((tm, tn), F32)],
        compiler_params=_params(("parallel", "arbitrary", "arbitrary"), blocks,
                                extra=3 * _nbytes((tm, tn), F32)),
        name=name,
    )(*args, resid)


def _gated_kernel(*refs, has_prev, wo_transposed, wm_transposed):
    if has_prev:
        a_ref, wo_ref, h_ref, wm_ref, p_ref, o_ref = refs
    else:
        a_ref, wo_ref, h_ref, wm_ref, o_ref = refs
    a = a_ref[...]
    h = h_ref[...]
    y = _mxu(a, wo_ref[...].astype(a.dtype), wo_transposed)
    g = jax.nn.sigmoid(_mxu(h, wm_ref[...].astype(h.dtype), wm_transposed))
    r = g * y
    if has_prev:
        r = p_ref[...] + r
    o_ref[...] = r.astype(o_ref.dtype)


def gated_mm(a, wo, h, wm, n, prev, out_dtype, name):
    m, ka = a.shape
    kh = h.shape[1]
    tm, tn = _tile(m, 1024), _tile(n, 256)
    has_prev = prev is not None
    ident = lambda j: j
    in_specs = [pl.BlockSpec((tm, ka), lambda i, j: (i, 0)),
                _w_spec(wo, ka, tn, ident),
                pl.BlockSpec((tm, kh), lambda i, j: (i, 0)),
                _w_spec(wm, kh, tn, ident)]
    blocks = [((tm, ka), a.dtype), _w_block(wo, ka, tn), ((tm, kh), h.dtype), _w_block(wm, kh, tn),
              ((tm, tn), out_dtype)]
    args = [a, wo.arr, h, wm.arr]
    if has_prev:
        in_specs.append(pl.BlockSpec((tm, tn), lambda i, j: (i, j)))
        blocks.append(((tm, tn), prev.dtype))
        args.append(prev)
    return pl.pallas_call(
        functools.partial(_gated_kernel, has_prev=has_prev, wo_transposed=wo.row0 is not None,
                          wm_transposed=wm.row0 is not None),
        grid=(m // tm, n // tn),
        in_specs=in_specs,
        out_specs=pl.BlockSpec((tm, tn), lambda i, j: (i, j)),
        out_shape=jax.ShapeDtypeStruct((m, n), out_dtype),
        compiler_params=_params(("parallel", "arbitrary"), blocks, extra=4 * _nbytes((tm, tn), F32)),
        name=name,
    )(*args)


def _gelu_tanh(x):
    return 0.5 * x * (1.0 + jnp.tanh(np.sqrt(2.0 / np.pi).astype(np.float32) * (x + 0.044715 * (x * x * x))))


def _compress_kernel(a_ref, pa_ref, pb_ref, w1a_ref, w1b_ref, w2_ref, g_ref, o_ref, *, norm):
    nch = a_ref.shape[0] // CMP_STRIDE
    a = jnp.concatenate([a_ref[pl.ds(tok, nch, stride=CMP_STRIDE), :] for tok in range(CMP_STRIDE)], axis=1)
    ua = jnp.dot((a + pa_ref[...]).astype(MXU_DTYPE), w1a_ref[...], preferred_element_type=F32)
    ub = jnp.dot((a + pb_ref[...]).astype(MXU_DTYPE), w1b_ref[...], preferred_element_type=F32)
    pre = ua + pltpu.roll(ub, nch - 1, 0)
    hid = _gelu_tanh(pre).astype(MXU_DTYPE)
    out = jnp.dot(hid, w2_ref[...], preferred_element_type=F32)
    if norm:
        ms = jnp.mean(out * out, axis=-1, keepdims=True)
        out = out * lax.rsqrt(ms + EPS) * g_ref[...]
    o_ref[0] = out.astype(o_ref.dtype)


def compress(kv, head0, pos, w1, w2, gain, norm, name):
    t = kv.shape[0]
    hkv, nch, cw = N_KV_HEADS, t // CMP_STRIDE, CMP_STRIDE * HEAD_DIM
    hid = w1.shape[1]
    posf = pos.reshape(1, CMP_BLOCK * HEAD_DIM).astype(F32)
    pa, pb = posf[:, :cw], posf[:, cw:]
    w1c = w1.astype(MXU_DTYPE)
    w1a, w1b = w1c[:cw], w1c[cw:]
    full = lambda shape: pl.BlockSpec(shape, lambda h: (0,) * len(shape))
    return pl.pallas_call(
        functools.partial(_compress_kernel, norm=norm),
        grid=(hkv,),
        in_specs=[pl.BlockSpec((t, HEAD_DIM), lambda h: (0, head0 + h)),
                  full((1, cw)), full((1, cw)), full((cw, hid)), full((cw, hid)),
                  full((hid, HEAD_DIM)), full((1, HEAD_DIM))],
        out_specs=pl.BlockSpec((1, nch, HEAD_DIM), lambda h: (h, 0, 0)),
        out_shape=jax.ShapeDtypeStruct((hkv, nch, HEAD_DIM), MXU_DTYPE),
        compiler_params=_params(("parallel",),
                                [((nch, cw), F32), ((cw, hid), MXU_DTYPE), ((cw, hid), MXU_DTYPE)],
                                extra=8 * _nbytes((nch, cw), F32)),
        name=name,
    )(kv, pa, pb, w1a, w1b, w2.astype(MXU_DTYPE), gain.reshape(1, HEAD_DIM).astype(F32))


def _conv_kernel(cur_ref, prev_ref, cw_ref, cb_ref, lg_ref, lb_ref, o_ref, ext_ref, xs_ref, y_ref, *,
                 halo, rc, cc):
    i = pl.program_id(0)
    tm, ch = cur_ref.shape
    sub = 8
    ext_ref[0:halo, :] = jnp.where(i > 0, prev_ref[tm - halo:tm, :], 0.0)
    ext_ref[halo:halo + tm, :] = cur_ref[...]
    base = halo - (CONV_WIDTH - 1)
    span = xs_ref.shape[1]
    for c0 in range(0, ch, cc):
        for r in range(1, sub):
            xs_ref[r - 1] = ext_ref[r:r + span, c0:c0 + cc]
        for r0 in range(0, tm, rc):
            acc = jnp.zeros((rc, cc), F32) + cb_ref[:, c0:c0 + cc]
            for w in range(CONV_WIDTH):
                a, r = divmod(base + r0 + w, sub)
                if r == 0:
                    tap = ext_ref[a * sub:a * sub + rc, c0:c0 + cc]
                else:
                    tap = xs_ref[r - 1, a * sub:a * sub + rc, :]
                acc = acc + tap * cw_ref[w:w + 1, c0:c0 + cc]
            y_ref[r0:r0 + rc, c0:c0 + cc] = acc
    y = y_ref[...]
    mu = jnp.mean(y, axis=-1, keepdims=True)
    d = y - mu
    var = jnp.mean(d * d, axis=-1, keepdims=True)
    z = d * lax.rsqrt(var + EPS) * lg_ref[...] + lb_ref[...]
    o_ref[...] = (z * jax.nn.sigmoid(z)).astype(o_ref.dtype)


def conv_module(u, conv_w, conv_b, ln_g, ln_b):
    t, ch = u.shape
    tm = _tile(t, 256)
    halo = 32
    row = lambda v: v.reshape(1, ch).astype(F32)
    full = lambda shape: pl.BlockSpec(shape, lambda i: (0, 0))
    cc = min(512, ch)
    return pl.pallas_call(
        functools.partial(_conv_kernel, halo=halo, rc=64, cc=cc),
        grid=(t // tm,),
        in_specs=[pl.BlockSpec((tm, ch), lambda i: (i, 0)),
                  pl.BlockSpec((tm, ch), lambda i: (jnp.maximum(i - 1, 0), 0)),
                  full((CONV_WIDTH, ch)), full((1, ch)), full((1, ch)), full((1, ch))],
        out_specs=pl.BlockSpec((tm, ch), lambda i: (i, 0)),
        out_shape=jax.ShapeDtypeStruct((t, ch), MXU_DTYPE),
        scratch_shapes=[pltpu.VMEM((tm + halo, ch), F32), pltpu.VMEM((7, tm + halo - 8, cc), F32),
                        pltpu.VMEM((tm, ch), F32)],
        compiler_params=_params(("parallel",),
                                [((tm, ch), F32), ((tm, ch), F32), ((tm, ch), MXU_DTYPE)],
                                extra=6 * _nbytes((tm, ch), F32)),
        name="conv_module",
    )(u, u, conv_w.astype(F32), row(conv_b), row(ln_g), row(ln_b))


def _dot_t(a, b):
    return lax.dot_general(a, b, (((0,), (0,)), ((), ())), preferred_element_type=F32)


def _sort_desc(c):
    n = len(c)
    if n == 1:
        return list(c)
    return _merge_desc(_sort_desc(c[:n // 2]) + _sort_desc(c[n // 2:])[::-1])


def _merge_desc(c):
    c = list(c)
    n = len(c)
    d = n // 2
    while d >= 1:
        for a in range(n):
            if a & d == 0:
                c[a], c[a + d] = jnp.maximum(c[a], c[a + d]), jnp.minimum(c[a], c[a + d])
        d //= 2
    return c


def _kth_largest(score, k):
    ns = score.shape[0]
    sub = 8
    a = _sort_desc([score[j * sub:(j + 1) * sub, :] for j in range(ns // sub)])[:k]
    shift = sub // 2
    while True:
        b = [pltpu.roll(v, shift, 0) for v in a]
        if 2 * len(a) <= k:
            top = a + b[::-1]
        else:
            top = [jnp.maximum(a[j], b[len(a) - 1 - j]) for j in range(len(a))]
        if shift == 1:
            break
        a = _merge_desc(top)
        shift //= 2
    kth = top[0]
    for v in top[1:]:
        kth = jnp.minimum(kth, v)
    return kth[0:1, :]


def _nsa_kernel(q_ref, kc_ref, vc_ref, ks_ref, kw_ref, vs_ref, vw_ref, gt_ref, mt_ref, lt_ref, e_ref, o_ref,
                qx_ref, biasx_ref, m_ref, l_ref, acc_ref, out_ref, sa_ref, sb_ref, cs_ref, ws_ref, *,
                tq, n_top, wl):
    i = pl.program_id(1)
    t0 = i * tq
    nc = kc_ref.shape[1]
    ns = mt_ref.shape[0]
    nb = tq // SEL_BLOCK
    gsl = [slice(g * tq, (g + 1) * tq) for g in range(GROUP)]

    def gate(g, b):
        return gt_ref[0, g * 3 + b:g * 3 + b + 1, :]

    for g in range(GROUP):
        qg = q_ref[:, g * HEAD_DIM:(g + 1) * HEAD_DIM].astype(F32)
        qx_ref[0:HEAD_DIM, gsl[g]] = qg.T.astype(MXU_DTYPE)
    qx_ref[HEAD_DIM:, :] = jnp.zeros((HEAD_DIM, GROUP * tq), MXU_DTYPE)
    t_q = t0 + lax.broadcasted_iota(jnp.int32, (1, tq), 1)

    kc = kc_ref[0]
    vc = vc_ref[0]
    ws = pl.multiple_of(jnp.maximum(t0 + tq - wl, 0), SEL_BLOCK)
    kw = kw_ref[pl.ds(ws, wl), :]
    vw = vw_ref[pl.ds(ws, wl), :]
    for g in range(GROUP):
        cs_ref[g] = jnp.dot(kc, qx_ref[0:HEAD_DIM, gsl[g]], preferred_element_type=F32)
    for g in range(GROUP):
        ws_ref[g] = jnp.dot(kw, qx_ref[0:HEAD_DIM, gsl[g]], preferred_element_type=F32)

    cmp_end = lax.broadcasted_iota(jnp.int32, (nc, 1), 0) * CMP_STRIDE + (CMP_BLOCK - 1)
    cbias = jnp.where(cmp_end <= t_q, 0.0, NEG_INF)
    psum = None
    for g in range(GROUP):
        sm = cs_ref[g] + cbias
        mx = jnp.maximum(jnp.max(sm, axis=0, keepdims=True), 0.5 * NEG_INF)
        e = jnp.exp2(sm - mx)
        den = jnp.sum(e, axis=0, keepdims=True)
        p = e * (1.0 / jnp.where(den > 0.0, den, 1.0))
        out_ref[:, gsl[g]] = gate(g, 0) * _dot_t(vc, p.astype(MXU_DTYPE))
        psum = p if psum is None else psum + p

    p_hi = psum.astype(MXU_DTYPE)
    p_lo = (psum - p_hi.astype(F32)).astype(MXU_DTYPE)
    imp = (jnp.dot(mt_ref[...], p_hi, preferred_element_type=F32)
           + jnp.dot(mt_ref[...], p_lo, preferred_element_type=F32))
    s_id = lax.broadcasted_iota(jnp.int32, (ns, 1), 0)
    cur = t_q // SEL_BLOCK
    forced = (s_id == 0) | (s_id == cur) | (s_id == cur - 1)
    valid = s_id * SEL_BLOCK <= t_q
    score = jnp.where(forced, FORCE, jnp.where(valid, imp, NEG_INF))
    kth = _kth_largest(score, n_top)
    above = score > kth
    equal = score == kth
    n_above = jnp.sum(jnp.where(above, 1.0, 0.0), axis=0, keepdims=True)
    n_equal_upto = jnp.dot(lt_ref[...], jnp.where(equal, 1.0, 0.0).astype(MXU_DTYPE),
                           preferred_element_type=F32)
    picked = above | (equal & (n_equal_upto <= n_top - n_above))
    block_bias = jnp.where(picked, 0.0, NEG_INF)
    pad_rows = jnp.zeros((biasx_ref.shape[1] - nb, tq), F32)
    for j in range(ns // nb):
        slab = jnp.concatenate([block_bias[j * nb:(j + 1) * nb, :], pad_rows], axis=0)
        biasx_ref[j] = jnp.concatenate([slab] * GROUP, axis=1).astype(MXU_DTYPE)

    m_ref[...] = jnp.full(m_ref.shape, NEG_INF, F32)
    l_ref[...] = jnp.zeros(l_ref.shape, F32)
    acc_ref[...] = jnp.zeros(acc_ref.shape, F32)

    def issue_scores(kt, s_ref):
        k0 = pl.multiple_of(kt * tq, tq)
        kx = jnp.concatenate([ks_ref[pl.ds(k0, tq), :], e_ref[...]], axis=1)
        qx_ref[HEAD_DIM:HEAD_DIM + biasx_ref.shape[1], :] = biasx_ref[kt]
        for g in range(GROUP):
            s_ref[g] = jnp.dot(kx, qx_ref[:, gsl[g]], preferred_element_type=F32)

    def consume(kt, s_ref, diagonal):
        k0 = pl.multiple_of(kt * tq, tq)
        vv = vs_ref[pl.ds(k0, tq), :]
        if diagonal:
            causal = k0 + lax.broadcasted_iota(jnp.int32, (tq, 1), 0) <= t_q
        for g in range(GROUP):
            sm = jnp.where(causal, s_ref[g], NEG_INF) if diagonal else s_ref[g]
            m_old = m_ref[:, gsl[g]]
            m_new = jnp.maximum(m_old, jnp.max(sm, axis=0, keepdims=True))
            alpha = jnp.exp2(m_old - m_new)
            pt = jnp.exp2(sm - m_new)
            l_ref[:, gsl[g]] = alpha * l_ref[:, gsl[g]] + jnp.sum(pt, axis=0, keepdims=True)
            acc_ref[:, gsl[g]] = alpha * acc_ref[:, gsl[g]] + _dot_t(vv, pt.astype(MXU_DTYPE))
            m_ref[:, gsl[g]] = m_new

    def stage(kt, s_cur, s_next):
        issue_scores(kt + 1, s_next)
        consume(kt, s_cur, False)

    issue_scores(0, sa_ref)

    back = t_q - (ws + lax.broadcasted_iota(jnp.int32, (wl, 1), 0))
    wbias = jnp.where(lax.bitcast_convert_type(back, jnp.uint32) < WINDOW, 0.0, NEG_INF)
    for g in range(GROUP):
        sm = ws_ref[g] + wbias
        e = jnp.exp2(sm - jnp.max(sm, axis=0, keepdims=True))
        inv = 1.0 / jnp.sum(e, axis=0, keepdims=True)
        out_ref[:, gsl[g]] += (gate(g, 2) * inv) * _dot_t(vw, e.astype(MXU_DTYPE))

    def pair(j, carry):
        stage(2 * j, sa_ref, sb_ref)
        stage(2 * j + 1, sb_ref, sa_ref)
        return carry

    lax.fori_loop(0, i // 2, pair, 0)

    @pl.when(i % 2 == 0)
    def _():
        consume(i, sa_ref, True)

    @pl.when(i % 2 == 1)
    def _():
        stage(i - 1, sa_ref, sb_ref)
        consume(i, sb_ref, True)

    for g in range(GROUP):
        out_t = out_ref[:, gsl[g]] + (gate(g, 1) / l_ref[:, gsl[g]]) * acc_ref[:, gsl[g]]
        o_ref[:, g * HEAD_DIM:(g + 1) * HEAD_DIM] = out_t.T.astype(o_ref.dtype)


def nsa_attention(q, k_cmp, v_cmp, k_sw, v_sw, gates_t, tq=256):
    t = q.shape[0]
    tq = min(tq, t)
    wl = min(WINDOW + tq, t)
    nc = k_cmp.shape[1]
    ns = t // SEL_BLOCK
    n_top = min(N_SELECT, ns)
    r = GROUP * tq
    lt = jnp.asarray(np.tril(np.ones((ns, ns))), MXU_DTYPE)
    cj = np.arange(nc)[None, :] * CMP_STRIDE
    si = np.arange(ns)[:, None] * SEL_BLOCK
    ov = np.clip(np.minimum(cj + CMP_BLOCK, si + SEL_BLOCK) - np.maximum(cj, si), 0, None) / CMP_BLOCK
    ov[:, (t - CMP_BLOCK) // CMP_STRIDE + 1:] = 0.0
    mt = jnp.asarray(ov, MXU_DTYPE)
    nb = tq // SEL_BLOCK
    onehot = jnp.asarray(np.arange(tq)[:, None] // SEL_BLOCK == np.arange(HEAD_DIM)[None, :], MXU_DTYPE)
    bias_rows = 16
    assert nb <= bias_rows and ns % nb == 0

    head = lambda off: pl.BlockSpec((t, HEAD_DIM), lambda h, i: (0, off + h))
    return pl.pallas_call(
        functools.partial(_nsa_kernel, tq=tq, n_top=n_top, wl=wl),
        grid=(N_KV_HEADS, t // tq),
        in_specs=[pl.BlockSpec((tq, GROUP * HEAD_DIM), lambda h, i: (i, h)),
                  pl.BlockSpec((1, nc, HEAD_DIM), lambda h, i: (h, 0, 0)),
                  pl.BlockSpec((1, nc, HEAD_DIM), lambda h, i: (h, 0, 0)),
                  head(0), head(N_KV_HEADS), head(0), head(N_KV_HEADS),
                  pl.BlockSpec((1, 16, tq), lambda h, i: (h, 0, i)),
                  pl.BlockSpec((ns, nc), lambda h, i: (0, 0)),
                  pl.BlockSpec((ns, ns), lambda h, i: (0, 0)),
                  pl.BlockSpec((tq, HEAD_DIM), lambda h, i: (0, 0))],
        out_specs=pl.BlockSpec((tq, GROUP * HEAD_DIM), lambda h, i: (i, h)),
        out_shape=jax.ShapeDtypeStruct((t, Q_W), MXU_DTYPE),
        scratch_shapes=[pltpu.VMEM((2 * HEAD_DIM, r), MXU_DTYPE),
                        pltpu.VMEM((ns // nb, bias_rows, r), MXU_DTYPE),
                        pltpu.VMEM((1, r), F32), pltpu.VMEM((1, r), F32),
                        pltpu.VMEM((HEAD_DIM, r), F32), pltpu.VMEM((HEAD_DIM, r), F32),
                        pltpu.VMEM((GROUP, tq, tq), F32), pltpu.VMEM((GROUP, tq, tq), F32),
                        pltpu.VMEM((GROUP, nc, tq), F32), pltpu.VMEM((GROUP, wl, tq), F32)],
        compiler_params=_params(("parallel", "arbitrary"),
                                [((t, HEAD_DIM), MXU_DTYPE)] * 4 + [((nc, HEAD_DIM), MXU_DTYPE)] * 2
                                + [((tq, GROUP * HEAD_DIM), MXU_DTYPE)] * 2 + [((ns, nc), MXU_DTYPE)],
                                extra=12 * _nbytes((max(nc, wl), tq), F32)),
        name="nsa_attention",
    )(q, k_cmp, v_cmp, k_sw, k_sw, v_sw, v_sw, gates_t, mt, lt, onehot)


def _ffn(x, norm_g, w_gate, w_up, w_down):
    f = w_gate.shape[1]
    xn = rmsnorm_cast(x, norm_g)
    h, wd = dual_mm(xn, Weight(w_gate), Weight(w_up), f, jax.nn.silu, MXU_DTYPE, "ffn_gate_up", tm=2048, tn=256,
                    cast=w_down, x_buffers=1)
    return mm_resid(h, wd, x, 0.5, 4, "ffn_down")


def _layer(x, p):
    t, d = x.shape
    x = _ffn(x, p["ffn1_norm"], p["ffn1_w_gate"], p["ffn1_w_up"], p["ffn1_w_down"])

    hn = rmsnorm_cast(x, p["mix_norm"])
    w_in = p["w_in"]
    conv_ch = p["conv_w"].shape[1]
    assert Q_W % KV_W == 0
    qb = Q_W // KV_W
    w_t = w_in.T
    qkv = Weight(w_t, 0)
    o = Q_W + 6 * KV_W
    assert o % 8 == 0 and NSA_GATE_W % 8 == 0 and conv_ch % 8 == 0 and w_t.shape[0] >= o + LANES
    w_gn = Weight(w_t, o); o += NSA_GATE_W
    w_ga = Weight(w_t, o); o += conv_ch
    w_gg = Weight(w_t, o); o += conv_ch
    w_m0 = Weight(w_t, o); o += d
    w_m1 = Weight(w_t, o); o += d

    scale = HEAD_DIM ** -0.5 * np.log2(np.e)
    q = mm(hn, qkv, Q_W, "headnorm", MXU_DTYPE, "proj_q", gain=jnp.tile(p["q_norm"] * scale, N_HEADS))
    k_sw = mm(hn, qkv, 2 * KV_W, "headnorm", MXU_DTYPE, "proj_k", block_of=lambda j: qb + 2 + 2 * j,
              gain=jnp.concatenate([jnp.tile(p["k_norm"][1], N_KV_HEADS), jnp.tile(p["k_norm"][2], N_KV_HEADS)]))
    v_sw = mm(hn, qkv, 2 * KV_W, "plain", MXU_DTYPE, "proj_v", block_of=lambda j: qb + 3 + 2 * j)
    kv_c = mm(hn, qkv, 2 * KV_W, "plain", F32, "proj_kv_cmp", block_of=lambda j: qb + j)
    g_nsa = mm(hn, w_gn, LANES, "sigmoid", F32, "proj_gates")
    u = dual_mm(hn, w_gg, w_ga, conv_ch, jax.nn.sigmoid, F32, "proj_glu", tm=2048, tn=256, x_buffers=1)

    k_cmp = compress(kv_c, 0, p["cmp_pos_k"], p["cmp_k_w1"], p["cmp_k_w2"], p["k_norm"][0], True, "compress_k")
    v_cmp = compress(kv_c, N_KV_HEADS, p["cmp_pos_v"], p["cmp_v_w1"], p["cmp_v_w2"], p["k_norm"][0], False,
                     "compress_v")

    gates_t = g_nsa[:, :NSA_GATE_W].reshape(t, N_KV_HEADS, GROUP * 3).transpose(1, 2, 0)
    gates_t = jnp.pad(gates_t, ((0, 0), (0, 16 - GROUP * 3), (0, 0)))
    attn = nsa_attention(q, k_cmp, v_cmp, k_sw, v_sw, gates_t)

    uc = conv_module(u, p["conv_w"], p["conv_b"], p["conv_ln_g"], p["conv_ln_b"])

    mix = gated_mm(attn, Weight(p["nsa_w_o"]), hn, w_m0, d, None, F32, "merge_nsa")
    mix = gated_mm(uc, Weight(p["conv_w_o"]), hn, w_m1, d, mix, MXU_DTYPE, "merge_conv")
    x = mm_resid(mix, p["w_out"], x, 1.0, 1, "mix_out", tm=2048, tn=256, h_buffers=1)

    return _ffn(x, p["ffn2_norm"], p["ffn2_w_gate"], p["ffn2_w_up"], p["ffn2_w_down"])


_NAMES = ("ffn1_norm", "ffn1_w_gate", "ffn1_w_up", "ffn1_w_down", "mix_norm", "w_in", "q_norm", "k_norm",
          "cmp_pos_k", "cmp_k_w1", "cmp_k_w2", "cmp_pos_v", "cmp_v_w1", "cmp_v_w2", "nsa_w_o",
          "conv_w", "conv_b", "conv_ln_g", "conv_ln_b", "conv_w_o", "w_out",
          "ffn2_norm", "ffn2_w_gate", "ffn2_w_up", "ffn2_w_down")


def kernel(x, ffn1_norm, ffn1_w_gate, ffn1_w_up, ffn1_w_down, mix_norm, w_in, q_norm, k_norm, cmp_pos_k, cmp_k_w1, cmp_k_w2, cmp_pos_v, cmp_v_w1, cmp_v_w2, nsa_w_o, conv_w, conv_b, conv_ln_g, conv_ln_b, conv_w_o, w_out, ffn2_norm, ffn2_w_gate, ffn2_w_up, ffn2_w_down):
    params = (ffn1_norm, ffn1_w_gate, ffn1_w_up, ffn1_w_down, mix_norm, w_in, q_norm, k_norm, cmp_pos_k,
              cmp_k_w1, cmp_k_w2, cmp_pos_v, cmp_v_w1, cmp_v_w2, nsa_w_o, conv_w, conv_b, conv_ln_g,
              conv_ln_b, conv_w_o, w_out, ffn2_norm, ffn2_w_gate, ffn2_w_up, ffn2_w_down)
    b, t, d = x.shape
    depth = ffn1_norm.shape[0]
    xs = x.reshape(b * t, d)
    outs = []
    for bi in range(b):
        xb = xs[bi * t:(bi + 1) * t]
        for l in range(depth):
            xb = _layer(xb, {n: a[l] for n, a in zip(_NAMES, params)})
        outs.append(xb)
    out = outs[0] if b == 1 else jnp.concatenate(outs, axis=0)
    return out.reshape(b, t, d)
```

```python
import functools
from typing import NamedTuple, Optional

import numpy as np
import jax
import jax.numpy as jnp
from jax import lax
from jax.experimental import pallas as pl
from jax.experimental.pallas import tpu as pltpu

HEAD_DIM = 128
N_HEADS = 16
N_KV_HEADS = 4
GROUP = N_HEADS // N_KV_HEADS
CMP_BLOCK = 32
CMP_STRIDE = 16
SEL_BLOCK = 64
N_SELECT = 16
WINDOW = 512
CONV_WIDTH = 31
NEG_INF = -1e30
FORCE = 1e9
EPS = 1e-6

Q_W = N_HEADS * HEAD_DIM
KV_W = N_KV_HEADS * HEAD_DIM
NSA_GATE_W = N_HEADS * 3

MXU_DTYPE = jnp.bfloat16
LANES = 128
V7X_VMEM_BYTES = 64 * 1024 * 1024
VMEM_CAP = V7X_VMEM_BYTES - 6 * 1024 * 1024

F32 = jnp.float32


def _nbytes(shape, dtype):
    return int(np.prod(shape)) * jnp.dtype(dtype).itemsize


def _params(semantics, blocks, extra=0):
    need = 2 * sum(_nbytes(s, d) for s, d in blocks) + extra + (2 << 20)
    return pltpu.CompilerParams(dimension_semantics=semantics,
                                vmem_limit_bytes=int(min(max(need, 16 << 20), VMEM_CAP)))


def _tile(n, pref):
    return pref if n % pref == 0 else n


def _rmsnorm_kernel(x_ref, g_ref, o_ref):
    x = x_ref[...]
    ms = jnp.mean(x * x, axis=-1, keepdims=True)
    o_ref[...] = (x * lax.rsqrt(ms + EPS) * g_ref[...]).astype(o_ref.dtype)


def rmsnorm_cast(x, gain):
    m, d = x.shape
    tm = _tile(m, 256)
    return pl.pallas_call(
        _rmsnorm_kernel,
        grid=(m // tm,),
        in_specs=[pl.BlockSpec((tm, d), lambda i: (i, 0)),
                  pl.BlockSpec((1, d), lambda i: (0, 0))],
        out_specs=pl.BlockSpec((tm, d), lambda i: (i, 0)),
        out_shape=jax.ShapeDtypeStruct((m, d), MXU_DTYPE),
        compiler_params=_params(("parallel",), [((tm, d), F32), ((tm, d), MXU_DTYPE)],
                                extra=2 * _nbytes((tm, d), F32)),
        name="rmsnorm_cast",
    )(x, gain.reshape(1, d).astype(F32))


class Weight(NamedTuple):
    arr: jax.Array
    row0: Optional[int] = None


def _w_spec(w, k, tn, block_of):
    if w.row0 is None:
        return pl.BlockSpec((k, tn), lambda i, j: (0, block_of(j)))
    assert w.row0 % 8 == 0 and tn % 8 == 0
    return pl.BlockSpec((pl.Element(tn), pl.Element(k)),
                        lambda i, j: (pl.multiple_of(w.row0 + block_of(j) * tn, 8), 0))


def _w_block(w, k, tn):
    return ((k, tn) if w.row0 is None else (tn, k)), w.arr.dtype


def _mxu(x, w, transposed):
    if transposed:
        return lax.dot_general(x, w, (((1,), (1,)), ((), ())), preferred_element_type=F32)
    return jnp.dot(x, w, preferred_element_type=F32)


ROW_CHUNK = 512


def _row_chunked(rows, dots, epilogue):
    rc = min(rows, ROW_CHUNK)
    sl = [slice(c * rc, (c + 1) * rc) for c in range(rows // rc)]
    pending = dots(sl[0])
    for c, s in enumerate(sl):
        cur = pending
        if c + 1 < len(sl):
            pending = dots(sl[c + 1])
        epilogue(s, cur)


def _dual_kernel(*refs, act, transposed, has_cast):
    if has_cast:
        x_ref, w1_ref, w2_ref, c_ref, o_ref, co_ref = refs
        co_ref[...] = c_ref[...].astype(co_ref.dtype)
    else:
        x_ref, w1_ref, w2_ref, o_ref = refs
    w1 = w1_ref[...].astype(x_ref.dtype)
    w2 = w2_ref[...].astype(x_ref.dtype)

    def dots(s):
        x = x_ref[s, :]
        return _mxu(x, w1, transposed), _mxu(x, w2, transposed)

    def epilogue(s, ab):
        o_ref[s, :] = (act(ab[0]) * ab[1]).astype(o_ref.dtype)

    _row_chunked(x_ref.shape[0], dots, epilogue)


def dual_mm(x, w1, w2, n, act, out_dtype, name, tm=1024, tn=512, cast=None, x_buffers=None):
    m, k = x.shape
    tm, tn = _tile(m, tm), _tile(n, tn)
    assert n % tn == 0 and (w1.row0 is None) == (w2.row0 is None)
    block_of = lambda j: j
    ni, nj = m // tm, n // tn
    x_mode = {} if x_buffers is None else {"pipeline_mode": pl.Buffered(x_buffers)}
    in_specs = [pl.BlockSpec((tm, k), lambda i, j: (i, 0), **x_mode),
                _w_spec(w1, k, tn, block_of),
                _w_spec(w2, k, tn, block_of)]
    out_specs = pl.BlockSpec((tm, tn), lambda i, j: (i, j))
    out_shape = jax.ShapeDtypeStruct((m, n), out_dtype)
    blocks = [((tm, k), x.dtype), _w_block(w1, k, tn), _w_block(w2, k, tn), ((tm, tn), out_dtype)]
    args = [x, w1.arr, w2.arr]
    if cast is not None:
        rows, cols = cast.shape
        slab = rows // (ni * nj)
        assert slab * ni * nj == rows and slab % 16 == 0
        slab_spec = pl.BlockSpec((slab, cols), lambda i, j: (i * nj + j, 0))
        in_specs.append(slab_spec)
        out_specs = [out_specs, slab_spec]
        out_shape = [out_shape, jax.ShapeDtypeStruct((rows, cols), MXU_DTYPE)]
        blocks += [((slab, cols), cast.dtype), ((slab, cols), MXU_DTYPE)]
        args.append(cast)
    return pl.pallas_call(
        functools.partial(_dual_kernel, act=act, transposed=w1.row0 is not None, has_cast=cast is not None),
        grid=(ni, nj),
        in_specs=in_specs,
        out_specs=out_specs,
        out_shape=out_shape,
        compiler_params=_params(("parallel", "arbitrary"), blocks, extra=4 * _nbytes((tm, tn), F32)),
        name=name,
    )(*args)


def _mm_kernel(x_ref, w_ref, g_ref, o_ref, *, mode, transposed):
    def epilogue(s, acc):
        if mode == "headnorm":
            for c in range(acc.shape[1] // HEAD_DIM):
                sl = slice(c * HEAD_DIM, (c + 1) * HEAD_DIM)
                v = acc[:, sl]
                ms = jnp.mean(v * v, axis=-1, keepdims=True)
                o_ref[s, sl] = (v * lax.rsqrt(ms + EPS) * g_ref[:, sl]).astype(o_ref.dtype)
        elif mode == "sigmoid":
            o_ref[s, :] = jax.nn.sigmoid(acc).astype(o_ref.dtype)
        else:
            o_ref[s, :] = acc.astype(o_ref.dtype)

    w = w_ref[...].astype(x_ref.dtype)
    _row_chunked(x_ref.shape[0], lambda s: _mxu(x_ref[s, :], w, transposed), epilogue)


def mm(x, w, n, mode, out_dtype, name, gain=None, block_of=None):
    m, k = x.shape
    tm, tn = _tile(m, 1024), _tile(n, 512)
    if gain is None:
        gain = jnp.ones((1, n), F32)
    if block_of is None:
        block_of = lambda j: j
    return pl.pallas_call(
        functools.partial(_mm_kernel, mode=mode, transposed=w.row0 is not None),
        grid=(m // tm, n // tn),
        in_specs=[pl.BlockSpec((tm, k), lambda i, j: (i, 0)),
                  _w_spec(w, k, tn, block_of),
                  pl.BlockSpec((1, tn), lambda i, j: (0, j))],
        out_specs=pl.BlockSpec((tm, tn), lambda i, j: (i, j)),
        out_shape=jax.ShapeDtypeStruct((m, n), out_dtype),
        compiler_params=_params(("parallel", "arbitrary"),
                                [((tm, k), x.dtype), _w_block(w, k, tn), ((tm, tn), out_dtype)],
                                extra=3 * _nbytes((tm, tn), F32)),
        name=name,
    )(x, w.arr, gain.reshape(1, n).astype(F32))


def _mm_acc_kernel(*refs, scale, nk, has_tail):
    if has_tail:
        h_ref, w_ref, ht_ref, wt_ref, r_ref, o_ref, acc_ref = refs
    else:
        h_ref, w_ref, r_ref, o_ref, acc_ref = refs
    if nk == 1 and not has_tail:
        w = w_ref[...].astype(h_ref.dtype)

        def epilogue(s, acc):
            o_ref[s, :] = (r_ref[s, :] + scale * acc).astype(o_ref.dtype)

        _row_chunked(h_ref.shape[0], lambda s: _mxu(h_ref[s, :], w, False), epilogue)
        return
    kk = pl.program_id(2)

    @pl.when(kk == 0)
    def _():
        if has_tail:
            ht = ht_ref[...]
            acc_ref[...] = _mxu(ht, wt_ref[...].astype(ht.dtype), False)
        else:
            acc_ref[...] = jnp.zeros_like(acc_ref)

    h = h_ref[...]
    acc_ref[...] += _mxu(h, w_ref[...].astype(h.dtype), False)

    @pl.when(kk == nk - 1)
    def _():
        o_ref[...] = (r_ref[...] + scale * acc_ref[...]).astype(o_ref.dtype)


def _k_tiling(k, nk):
    if k % (nk * LANES) == 0:
        return k // nk, nk, 0
    tk = k // nk // LANES * LANES
    tail = k - nk * tk
    if tk > 0 and tail % LANES == 0 and (nk * tk) % tail == 0:
        return tk, nk, tail
    return k, 1, 0


def mm_resid(h, w, resid, scale, nk, name, tm=1024, tn=1024, h_buffers=None):
    m, k = h.shape
    n = w.shape[1]
    tm, tn = _tile(m, tm), _tile(n, tn)
    tk, nk, tail = _k_tiling(k, nk)
    h_mode = {} if h_buffers is None else {"pipeline_mode": pl.Buffered(h_buffers)}
    in_specs = [pl.BlockSpec((tm, tk), lambda i, j, kk: (i, kk), **h_mode),
                pl.BlockSpec((tk, tn), lambda i, j, kk: (kk, j))]
    blocks = [((tm, tk), h.dtype), ((tk, tn), w.dtype), ((tm, tn), F32), ((tm, tn), F32)]
    args = [h, w]
    if tail:
        tb = nk * tk // tail
        in_specs += [pl.BlockSpec((tm, tail), lambda i, j, kk: (i, tb)),
                     pl.BlockSpec((tail, tn), lambda i, j, kk: (tb, j))]
        blocks += [((tm, tail), h.dtype), ((tail, tn), w.dtype)]
        args += [h, w]
    in_specs.append(pl.BlockSpec((tm, tn), lambda i, j, kk: (i, j)))
    return pl.pallas_call(
        functools.partial(_mm_acc_kernel, scale=scale, nk=nk, has_tail=bool(tail)),
        grid=(m // tm, n // tn, nk),
        in_specs=in_specs,
        out_specs=pl.BlockSpec((tm, tn), lambda i, j, kk: (i, j)),
        out_shape=jax.ShapeDtypeStruct((m, n), F32),
        scratch_shapes=[pltpu.VMEM((tm, tn), F32)],
        compiler_params=_params(("parallel", "arbitrary", "arbitrary"), blocks,
                                extra=3 * _nbytes((tm, tn), F32)),
        name=name,
    )(*args, resid)


def _gated_kernel(*refs, has_prev, wo_transposed, wm_transposed):
    if has_prev:
        a_ref, wo_ref, h_ref, wm_ref, p_ref, o_ref = refs
    else:
        a_ref, wo_ref, h_ref, wm_ref, o_ref = refs
    a = a_ref[...]
    h = h_ref[...]
    y = _mxu(a, wo_ref[...].astype(a.dtype), wo_transposed)
    g = jax.nn.sigmoid(_mxu(h, wm_ref[...].astype(h.dtype), wm_transposed))
    r = g * y
    if has_prev:
        r = p_ref[...] + r
    o_ref[...] = r.astype(o_ref.dtype)


def gated_mm(a, wo, h, wm, n, prev, out_dtype, name):
    m, ka = a.shape
    kh = h.shape[1]
    tm, tn = _tile(m, 1024), _tile(n, 256)
    has_prev = prev is not None
    ident = lambda j: j
    in_specs = [pl.BlockSpec((tm, ka), lambda i, j: (i, 0)),
                _w_spec(wo, ka, tn, ident),
                pl.BlockSpec((tm, kh), lambda i, j: (i, 0)),
                _w_spec(wm, kh, tn, ident)]
    blocks = [((tm, ka), a.dtype), _w_block(wo, ka, tn), ((tm, kh), h.dtype), _w_block(wm, kh, tn),
              ((tm, tn), out_dtype)]
    args = [a, wo.arr, h, wm.arr]
    if has_prev:
        in_specs.append(pl.BlockSpec((tm, tn), lambda i, j: (i, j)))
        blocks.append(((tm, tn), prev.dtype))
        args.append(prev)
    return pl.pallas_call(
        functools.partial(_gated_kernel, has_prev=has_prev, wo_transposed=wo.row0 is not None,
                          wm_transposed=wm.row0 is not None),
        grid=(m // tm, n // tn),
        in_specs=in_specs,
        out_specs=pl.BlockSpec((tm, tn), lambda i, j: (i, j)),
        out_shape=jax.ShapeDtypeStruct((m, n), out_dtype),
        compiler_params=_params(("parallel", "arbitrary"), blocks, extra=4 * _nbytes((tm, tn), F32)),
        name=name,
    )(*args)


def _merge2_kernel(a_ref, wa_ref, c_ref, wc_ref, h_ref, wm0_ref, wm1_ref, o_ref):
    a, c, h = a_ref[...], c_ref[...], h_ref[...]
    r = jax.nn.sigmoid(_mxu(h, wm0_ref[...].astype(h.dtype), True)) * _mxu(a, wa_ref[...].astype(a.dtype), False)
    r = r + jax.nn.sigmoid(_mxu(h, wm1_ref[...].astype(h.dtype), True)) * _mxu(c, wc_ref[...].astype(c.dtype), False)
    o_ref[...] = r.astype(o_ref.dtype)


def merge2(a, wa, c, wc, h, wm0, wm1, n, name):
    m, ka = a.shape
    kc, kh = c.shape[1], h.shape[1]
    tm, tn = _tile(m, 1024), _tile(n, 256)
    ident = lambda j: j
    once = pl.Buffered(1)
    row = lambda k: pl.BlockSpec((tm, k), lambda i, j: (i, 0), pipeline_mode=once)
    return pl.pallas_call(
        _merge2_kernel,
        grid=(m // tm, n // tn),
        in_specs=[row(ka), _w_spec(wa, ka, tn, ident), row(kc), _w_spec(wc, kc, tn, ident),
                  row(kh), _w_spec(wm0, kh, tn, ident), _w_spec(wm1, kh, tn, ident)],
        out_specs=pl.BlockSpec((tm, tn), lambda i, j: (i, j)),
        out_shape=jax.ShapeDtypeStruct((m, n), MXU_DTYPE),
        compiler_params=_params(("parallel", "arbitrary"),
                                [((tm, ka), a.dtype), ((tm, kc), c.dtype), ((tm, kh), h.dtype),
                                 _w_block(wa, ka, tn), _w_block(wc, kc, tn), _w_block(wm0, kh, tn),
                                 _w_block(wm1, kh, tn), ((tm, tn), MXU_DTYPE)],
                                extra=6 * _nbytes((tm, tn), F32)),
        name=name,
    )(a, wa.arr, c, wc.arr, h, wm0.arr, wm1.arr)


def _gelu_tanh(x):
    return 0.5 * x * (1.0 + jnp.tanh(np.sqrt(2.0 / np.pi).astype(np.float32) * (x + 0.044715 * (x * x * x))))


def _compress_kernel(a_ref, pa_ref, pb_ref, w1a_ref, w1b_ref, w2_ref, g_ref, o_ref, *, norm):
    nch = a_ref.shape[0] // CMP_STRIDE
    a = jnp.concatenate([a_ref[pl.ds(tok, nch, stride=CMP_STRIDE), :] for tok in range(CMP_STRIDE)], axis=1)
    ua = jnp.dot((a + pa_ref[...]).astype(MXU_DTYPE), w1a_ref[...], preferred_element_type=F32)
    ub = jnp.dot((a + pb_ref[...]).astype(MXU_DTYPE), w1b_ref[...], preferred_element_type=F32)
    pre = ua + pltpu.roll(ub, nch - 1, 0)
    hid = _gelu_tanh(pre).astype(MXU_DTYPE)
    out = jnp.dot(hid, w2_ref[...], preferred_element_type=F32)
    if norm:
        ms = jnp.mean(out * out, axis=-1, keepdims=True)
        out = out * lax.rsqrt(ms + EPS) * g_ref[...]
    o_ref[0] = out.astype(o_ref.dtype)


def compress(kv, head0, pos, w1, w2, gain, norm, name):
    t = kv.shape[0]
    hkv, nch, cw = N_KV_HEADS, t // CMP_STRIDE, CMP_STRIDE * HEAD_DIM
    hid = w1.shape[1]
    posf = pos.reshape(1, CMP_BLOCK * HEAD_DIM).astype(F32)
    pa, pb = posf[:, :cw], posf[:, cw:]
    w1c = w1.astype(MXU_DTYPE)
    w1a, w1b = w1c[:cw], w1c[cw:]
    full = lambda shape: pl.BlockSpec(shape, lambda h: (0,) * len(shape))
    return pl.pallas_call(
        functools.partial(_compress_kernel, norm=norm),
        grid=(hkv,),
        in_specs=[pl.BlockSpec((t, HEAD_DIM), lambda h: (0, head0 + h)),
                  full((1, cw)), full((1, cw)), full((cw, hid)), full((cw, hid)),
                  full((hid, HEAD_DIM)), full((1, HEAD_DIM))],
        out_specs=pl.BlockSpec((1, nch, HEAD_DIM), lambda h: (h, 0, 0)),
        out_shape=jax.ShapeDtypeStruct((hkv, nch, HEAD_DIM), MXU_DTYPE),
        compiler_params=_params(("parallel",),
                                [((nch, cw), F32), ((cw, hid), MXU_DTYPE), ((cw, hid), MXU_DTYPE)],
                                extra=8 * _nbytes((nch, cw), F32)),
        name=name,
    )(kv, pa, pb, w1a, w1b, w2.astype(MXU_DTYPE), gain.reshape(1, HEAD_DIM).astype(F32))


def _conv_kernel(cur_ref, prev_ref, cw_ref, cb_ref, lg_ref, lb_ref, o_ref, ext_ref, xs_ref, y_ref, *,
                 halo, rc, cc):
    i = pl.program_id(0)
    tm, ch = cur_ref.shape
    sub = 8
    ext_ref[0:halo, :] = jnp.where(i > 0, prev_ref[tm - halo:tm, :], 0.0)
    ext_ref[halo:halo + tm, :] = cur_ref[...]
    base = halo - (CONV_WIDTH - 1)
    span = xs_ref.shape[1]
    for c0 in range(0, ch, cc):
        for r in range(1, sub):
            xs_ref[r - 1] = ext_ref[r:r + span, c0:c0 + cc]
        for r0 in range(0, tm, rc):
            acc = jnp.zeros((rc, cc), F32) + cb_ref[:, c0:c0 + cc]
            for w in range(CONV_WIDTH):
                a, r = divmod(base + r0 + w, sub)
                if r == 0:
                    tap = ext_ref[a * sub:a * sub + rc, c0:c0 + cc]
                else:
                    tap = xs_ref[r - 1, a * sub:a * sub + rc, :]
                acc = acc + tap * cw_ref[w:w + 1, c0:c0 + cc]
            y_ref[r0:r0 + rc, c0:c0 + cc] = acc
    y = y_ref[...]
    mu = jnp.mean(y, axis=-1, keepdims=True)
    d = y - mu
    var = jnp.mean(d * d, axis=-1, keepdims=True)
    z = d * lax.rsqrt(var + EPS) * lg_ref[...] + lb_ref[...]
    o_ref[...] = (z * jax.nn.sigmoid(z)).astype(o_ref.dtype)


def conv_module(u, conv_w, conv_b, ln_g, ln_b):
    t, ch = u.shape
    tm = _tile(t, 256)
    halo = 32
    row = lambda v: v.reshape(1, ch).astype(F32)
    full = lambda shape: pl.BlockSpec(shape, lambda i: (0, 0))
    cc = min(512, ch)
    return pl.pallas_call(
        functools.partial(_conv_kernel, halo=halo, rc=64, cc=cc),
        grid=(t // tm,),
        in_specs=[pl.BlockSpec((tm, ch), lambda i: (i, 0)),
                  pl.BlockSpec((tm, ch), lambda i: (jnp.maximum(i - 1, 0), 0)),
                  full((CONV_WIDTH, ch)), full((1, ch)), full((1, ch)), full((1, ch))],
        out_specs=pl.BlockSpec((tm, ch), lambda i: (i, 0)),
        out_shape=jax.ShapeDtypeStruct((t, ch), MXU_DTYPE),
        scratch_shapes=[pltpu.VMEM((tm + halo, ch), F32), pltpu.VMEM((7, tm + halo - 8, cc), F32),
                        pltpu.VMEM((tm, ch), F32)],
        compiler_params=_params(("parallel",),
                                [((tm, ch), F32), ((tm, ch), F32), ((tm, ch), MXU_DTYPE)],
                                extra=6 * _nbytes((tm, ch), F32)),
        name="conv_module",
    )(u, u, conv_w.astype(F32), row(conv_b), row(ln_g), row(ln_b))


def _dot_t(a, b):
    return lax.dot_general(a, b, (((0,), (0,)), ((), ())), preferred_element_type=F32)


def _sort_desc(c):
    n = len(c)
    if n == 1:
        return list(c)
    return _merge_desc(_sort_desc(c[:n // 2]) + _sort_desc(c[n // 2:])[::-1])


def _merge_desc(c):
    c = list(c)
    n = len(c)
    d = n // 2
    while d >= 1:
        for a in range(n):
            if a & d == 0:
                c[a], c[a + d] = jnp.maximum(c[a], c[a + d]), jnp.minimum(c[a], c[a + d])
        d //= 2
    return c


def _kth_largest(score, k):
    ns = score.shape[0]
    sub = 8
    a = _sort_desc([score[j * sub:(j + 1) * sub, :] for j in range(ns // sub)])[:k]
    shift = sub // 2
    while True:
        b = [pltpu.roll(v, shift, 0) for v in a]
        if 2 * len(a) <= k:
            top = a + b[::-1]
        else:
            top = [jnp.maximum(a[j], b[len(a) - 1 - j]) for j in range(len(a))]
        if shift == 1:
            break
        a = _merge_desc(top)
        shift //= 2
    kth = top[0]
    for v in top[1:]:
        kth = jnp.minimum(kth, v)
    return kth[0:1, :]


def _nsa_kernel(q_ref, kc_ref, vc_ref, ks_ref, kw_ref, vs_ref, vw_ref, gt_ref, mt_ref, lt_ref, e_ref, o_ref,
                qx_ref, biasx_ref, m_ref, l_ref, acc_ref, out_ref, sa_ref, sb_ref, cs_ref, ws_ref, *,
                tq, n_top, wl):
    i = pl.program_id(1)
    t0 = i * tq
    nc = kc_ref.shape[1]
    ns = mt_ref.shape[0]
    nb = tq // SEL_BLOCK
    gsl = [slice(g * tq, (g + 1) * tq) for g in range(GROUP)]

    def gate(g, b):
        return gt_ref[0, g * 3 + b:g * 3 + b + 1, :]

    for g in range(GROUP):
        qg = q_ref[:, g * HEAD_DIM:(g + 1) * HEAD_DIM].astype(F32)
        qx_ref[0:HEAD_DIM, gsl[g]] = qg.T.astype(MXU_DTYPE)
    qx_ref[HEAD_DIM:, :] = jnp.zeros((HEAD_DIM, GROUP * tq), MXU_DTYPE)
    t_q = t0 + lax.broadcasted_iota(jnp.int32, (1, tq), 1)

    kc = kc_ref[0]
    vc = vc_ref[0]
    ws = pl.multiple_of(jnp.maximum(t0 + tq - wl, 0), SEL_BLOCK)
    kw = kw_ref[pl.ds(ws, wl), :]
    vw = vw_ref[pl.ds(ws, wl), :]
    for g in range(GROUP):
        cs_ref[g] = jnp.dot(kc, qx_ref[0:HEAD_DIM, gsl[g]], preferred_element_type=F32)
    for g in range(GROUP):
        ws_ref[g] = jnp.dot(kw, qx_ref[0:HEAD_DIM, gsl[g]], preferred_element_type=F32)

    cmp_end = lax.broadcasted_iota(jnp.int32, (nc, 1), 0) * CMP_STRIDE + (CMP_BLOCK - 1)
    cbias = jnp.where(cmp_end <= t_q, 0.0, NEG_INF)
    psum = None
    for g in range(GROUP):
        sm = cs_ref[g] + cbias
        mx = jnp.maximum(jnp.max(sm, axis=0, keepdims=True), 0.5 * NEG_INF)
        e = jnp.exp2(sm - mx)
        den = jnp.sum(e, axis=0, keepdims=True)
        p = e * (1.0 / jnp.where(den > 0.0, den, 1.0))
        out_ref[:, gsl[g]] = gate(g, 0) * _dot_t(vc, p.astype(MXU_DTYPE))
        psum = p if psum is None else psum + p

    p_hi = psum.astype(MXU_DTYPE)
    p_lo = (psum - p_hi.astype(F32)).astype(MXU_DTYPE)
    imp = (jnp.dot(mt_ref[...], p_hi, preferred_element_type=F32)
           + jnp.dot(mt_ref[...], p_lo, preferred_element_type=F32))
    s_id = lax.broadcasted_iota(jnp.int32, (ns, 1), 0)
    cur = t_q // SEL_BLOCK
    forced = (s_id == 0) | (s_id == cur) | (s_id == cur - 1)
    valid = s_id * SEL_BLOCK <= t_q
    score = jnp.where(forced, FORCE, jnp.where(valid, imp, NEG_INF))
    kth = _kth_largest(score, n_top)
    above = score > kth
    equal = score == kth
    n_above = jnp.sum(jnp.where(above, 1.0, 0.0), axis=0, keepdims=True)
    n_equal_upto = jnp.dot(lt_ref[...], jnp.where(equal, 1.0, 0.0).astype(MXU_DTYPE),
                           preferred_element_type=F32)
    picked = above | (equal & (n_equal_upto <= n_top - n_above))
    block_bias = jnp.where(picked, 0.0, NEG_INF)
    pad_rows = jnp.zeros((biasx_ref.shape[1] - nb, tq), F32)
    for j in range(ns // nb):
        slab = jnp.concatenate([block_bias[j * nb:(j + 1) * nb, :], pad_rows], axis=0)
        biasx_ref[j] = jnp.concatenate([slab] * GROUP, axis=1).astype(MXU_DTYPE)

    m_ref[...] = jnp.full(m_ref.shape, NEG_INF, F32)
    l_ref[...] = jnp.zeros(l_ref.shape, F32)
    acc_ref[...] = jnp.zeros(acc_ref.shape, F32)

    def issue_scores(kt, s_ref):
        k0 = pl.multiple_of(kt * tq, tq)
        kx = jnp.concatenate([ks_ref[pl.ds(k0, tq), :], e_ref[...]], axis=1)
        qx_ref[HEAD_DIM:HEAD_DIM + biasx_ref.shape[1], :] = biasx_ref[kt]
        for g in range(GROUP):
            s_ref[g] = jnp.dot(kx, qx_ref[:, gsl[g]], preferred_element_type=F32)

    def consume(kt, s_ref, diagonal):
        k0 = pl.multiple_of(kt * tq, tq)
        vv = vs_ref[pl.ds(k0, tq), :]
        if diagonal:
            causal = k0 + lax.broadcasted_iota(jnp.int32, (tq, 1), 0) <= t_q
        for g in range(GROUP):
            sm = jnp.where(causal, s_ref[g], NEG_INF) if diagonal else s_ref[g]
            m_old = m_ref[:, gsl[g]]
            m_new = jnp.maximum(m_old, jnp.max(sm, axis=0, keepdims=True))
            alpha = jnp.exp2(m_old - m_new)
            pt = jnp.exp2(sm - m_new)
            l_ref[:, gsl[g]] = alpha * l_ref[:, gsl[g]] + jnp.sum(pt, axis=0, keepdims=True)
            acc_ref[:, gsl[g]] = alpha * acc_ref[:, gsl[g]] + _dot_t(vv, pt.astype(MXU_DTYPE))
            m_ref[:, gsl[g]] = m_new

    def stage(kt, s_cur, s_next):
        issue_scores(kt + 1, s_next)
        consume(kt, s_cur, False)

    issue_scores(0, sa_ref)

    back = t_q - (ws + lax.broadcasted_iota(jnp.int32, (wl, 1), 0))
    wbias = jnp.where(lax.bitcast_convert_type(back, jnp.uint32) < WINDOW, 0.0, NEG_INF)
    for g in range(GROUP):
        sm = ws_ref[g] + wbias
        e = jnp.exp2(sm - jnp.max(sm, axis=0, keepdims=True))
        inv = 1.0 / jnp.sum(e, axis=0, keepdims=True)
        out_ref[:, gsl[g]] += (gate(g, 2) * inv) * _dot_t(vw, e.astype(MXU_DTYPE))

    def pair(j, carry):
        stage(2 * j, sa_ref, sb_ref)
        stage(2 * j + 1, sb_ref, sa_ref)
        return carry

    lax.fori_loop(0, i // 2, pair, 0)

    @pl.when(i % 2 == 0)
    def _():
        consume(i, sa_ref, True)

    @pl.when(i % 2 == 1)
    def _():
        stage(i - 1, sa_ref, sb_ref)
        consume(i, sb_ref, True)

    for g in range(GROUP):
        out_t = out_ref[:, gsl[g]] + (gate(g, 1) / l_ref[:, gsl[g]]) * acc_ref[:, gsl[g]]
        o_ref[:, g * HEAD_DIM:(g + 1) * HEAD_DIM] = out_t.T.astype(o_ref.dtype)


def nsa_attention(q, k_cmp, v_cmp, k_sw, v_sw, gates_t, tq=256):
    t = q.shape[0]
    tq = min(tq, t)
    wl = min(WINDOW + tq, t)
    nc = k_cmp.shape[1]
    ns = t // SEL_BLOCK
    n_top = min(N_SELECT, ns)
    r = GROUP * tq
    lt = jnp.asarray(np.tril(np.ones((ns, ns))), MXU_DTYPE)
    cj = np.arange(nc)[None, :] * CMP_STRIDE
    si = np.arange(ns)[:, None] * SEL_BLOCK
    ov = np.clip(np.minimum(cj + CMP_BLOCK, si + SEL_BLOCK) - np.maximum(cj, si), 0, None) / CMP_BLOCK
    ov[:, (t - CMP_BLOCK) // CMP_STRIDE + 1:] = 0.0
    mt = jnp.asarray(ov, MXU_DTYPE)
    nb = tq // SEL_BLOCK
    onehot = jnp.asarray(np.arange(tq)[:, None] // SEL_BLOCK == np.arange(HEAD_DIM)[None, :], MXU_DTYPE)
    bias_rows = 16
    assert nb <= bias_rows and ns % nb == 0

    head = lambda off: pl.BlockSpec((t, HEAD_DIM), lambda h, i: (0, off + h))
    return pl.pallas_call(
        functools.partial(_nsa_kernel, tq=tq, n_top=n_top, wl=wl),
        grid=(N_KV_HEADS, t // tq),
        in_specs=[pl.BlockSpec((tq, GROUP * HEAD_DIM), lambda h, i: (i, h)),
                  pl.BlockSpec((1, nc, HEAD_DIM), lambda h, i: (h, 0, 0)),
                  pl.BlockSpec((1, nc, HEAD_DIM), lambda h, i: (h, 0, 0)),
                  head(0), head(N_KV_HEADS), head(0), head(N_KV_HEADS),
                  pl.BlockSpec((1, 16, tq), lambda h, i: (h, 0, i)),
                  pl.BlockSpec((ns, nc), lambda h, i: (0, 0)),
                  pl.BlockSpec((ns, ns), lambda h, i: (0, 0)),
                  pl.BlockSpec((tq, HEAD_DIM), lambda h, i: (0, 0))],
        out_specs=pl.BlockSpec((tq, GROUP * HEAD_DIM), lambda h, i: (i, h)),
        out_shape=jax.ShapeDtypeStruct((t, Q_W), MXU_DTYPE),
        scratch_shapes=[pltpu.VMEM((2 * HEAD_DIM, r), MXU_DTYPE),
                        pltpu.VMEM((ns // nb, bias_rows, r), MXU_DTYPE),
                        pltpu.VMEM((1, r), F32), pltpu.VMEM((1, r), F32),
                        pltpu.VMEM((HEAD_DIM, r), F32), pltpu.VMEM((HEAD_DIM, r), F32),
                        pltpu.VMEM((GROUP, tq, tq), F32), pltpu.VMEM((GROUP, tq, tq), F32),
                        pltpu.VMEM((GROUP, nc, tq), F32), pltpu.VMEM((GROUP, wl, tq), F32)],
        compiler_params=_params(("parallel", "arbitrary"),
                                [((t, HEAD_DIM), MXU_DTYPE)] * 4 + [((nc, HEAD_DIM), MXU_DTYPE)] * 2
                                + [((tq, GROUP * HEAD_DIM), MXU_DTYPE)] * 2 + [((ns, nc), MXU_DTYPE)],
                                extra=12 * _nbytes((max(nc, wl), tq), F32)),
        name="nsa_attention",
    )(q, k_cmp, v_cmp, k_sw, k_sw, v_sw, v_sw, gates_t, mt, lt, onehot)


def _ffn(x, norm_g, w_gate, w_up, w_down):
    f = w_gate.shape[1]
    xn = rmsnorm_cast(x, norm_g)
    h, wd = dual_mm(xn, Weight(w_gate), Weight(w_up), f, jax.nn.silu, MXU_DTYPE, "ffn_gate_up", tm=2048, tn=256,
                    cast=w_down, x_buffers=1)
    return mm_resid(h, wd, x, 0.5, 4, "ffn_down")


def _layer(x, p):
    t, d = x.shape
    x = _ffn(x, p["ffn1_norm"], p["ffn1_w_gate"], p["ffn1_w_up"], p["ffn1_w_down"])

    hn = rmsnorm_cast(x, p["mix_norm"])
    w_in = p["w_in"]
    conv_ch = p["conv_w"].shape[1]
    assert Q_W % KV_W == 0
    qb = Q_W // KV_W
    w_t = w_in.T
    qkv = Weight(w_t, 0)
    o = Q_W + 6 * KV_W
    assert o % 8 == 0 and NSA_GATE_W % 8 == 0 and conv_ch % 8 == 0 and w_t.shape[0] >= o + LANES
    w_gn = Weight(w_t, o); o += NSA_GATE_W
    w_ga = Weight(w_t, o); o += conv_ch
    w_gg = Weight(w_t, o); o += conv_ch
    w_m0 = Weight(w_t, o); o += d
    w_m1 = Weight(w_t, o); o += d

    scale = HEAD_DIM ** -0.5 * np.log2(np.e)
    q = mm(hn, qkv, Q_W, "headnorm", MXU_DTYPE, "proj_q", gain=jnp.tile(p["q_norm"] * scale, N_HEADS))
    k_sw = mm(hn, qkv, 2 * KV_W, "headnorm", MXU_DTYPE, "proj_k", block_of=lambda j: qb + 2 + 2 * j,
              gain=jnp.concatenate([jnp.tile(p["k_norm"][1], N_KV_HEADS), jnp.tile(p["k_norm"][2], N_KV_HEADS)]))
    v_sw = mm(hn, qkv, 2 * KV_W, "plain", MXU_DTYPE, "proj_v", block_of=lambda j: qb + 3 + 2 * j)
    kv_c = mm(hn, qkv, 2 * KV_W, "plain", F32, "proj_kv_cmp", block_of=lambda j: qb + j)
    g_nsa = mm(hn, w_gn, LANES, "sigmoid", F32, "proj_gates")
    u = dual_mm(hn, w_gg, w_ga, conv_ch, jax.nn.sigmoid, F32, "proj_glu", tn=256)

    k_cmp = compress(kv_c, 0, p["cmp_pos_k"], p["cmp_k_w1"], p["cmp_k_w2"], p["k_norm"][0], True, "compress_k")
    v_cmp = compress(kv_c, N_KV_HEADS, p["cmp_pos_v"], p["cmp_v_w1"], p["cmp_v_w2"], p["k_norm"][0], False,
                     "compress_v")

    gates_t = g_nsa[:, :NSA_GATE_W].reshape(t, N_KV_HEADS, GROUP * 3).transpose(1, 2, 0)
    gates_t = jnp.pad(gates_t, ((0, 0), (0, 16 - GROUP * 3), (0, 0)))
    attn = nsa_attention(q, k_cmp, v_cmp, k_sw, v_sw, gates_t)

    uc = conv_module(u, p["conv_w"], p["conv_b"], p["conv_ln_g"], p["conv_ln_b"])

    mix = merge2(attn, Weight(p["nsa_w_o"]), uc, Weight(p["conv_w_o"]), hn, w_m0, w_m1, d, "merge_both")
    x = mm_resid(mix, p["w_out"], x, 1.0, 1, "mix_out", tm=2048, tn=256, h_buffers=1)

    return _ffn(x, p["ffn2_norm"], p["ffn2_w_gate"], p["ffn2_w_up"], p["ffn2_w_down"])


_NAMES = ("ffn1_norm", "ffn1_w_gate", "ffn1_w_up", "ffn1_w_down", "mix_norm", "w_in", "q_norm", "k_norm",
          "cmp_pos_k", "cmp_k_w1", "cmp_k_w2", "cmp_pos_v", "cmp_v_w1", "cmp_v_w2", "nsa_w_o",
          "conv_w", "conv_b", "conv_ln_g", "conv_ln_b", "conv_w_o", "w_out",
          "ffn2_norm", "ffn2_w_gate", "ffn2_w_up", "ffn2_w_down")


def kernel(x, ffn1_norm, ffn1_w_gate, ffn1_w_up, ffn1_w_down, mix_norm, w_in, q_norm, k_norm, cmp_pos_k, cmp_k_w1, cmp_k_w2, cmp_pos_v, cmp_v_w1, cmp_v_w2, nsa_w_o, conv_w, conv_b, conv_ln_g, conv_ln_b, conv_w_o, w_out, ffn2_norm, ffn2_w_gate, ffn2_w_up, ffn2_w_down):
    params = (ffn1_norm, ffn1_w_gate, ffn1_w_up, ffn1_w_down, mix_norm, w_in, q_norm, k_norm, cmp_pos_k,
              cmp_k_w1, cmp_k_w2, cmp_pos_v, cmp_v_w1, cmp_v_w2, nsa_w_o, conv_w, conv_b, conv_ln_g,
              conv_ln_b, conv_w_o, w_out, ffn2_norm, ffn2_w_gate, ffn2_w_up, ffn2_w_down)
    b, t, d = x.shape
    depth = ffn1_norm.shape[0]
    xs = x.reshape(b * t, d)
    outs = []
    for bi in range(b):
        xb = xs[bi * t:(bi + 1) * t]
        for l in range(depth):
            xb = _layer(xb, {n: a[l] for n, a in zip(_NAMES, params)})
        outs.append(xb)
    out = outs[0] if b == 1 else jnp.concatenate(outs, axis=0)
    return out.reshape(b, t, d)
```
